```python
import math
import jax, jax.numpy as jnp
from jax import lax
import numpy as np

D_MODEL = 2048
BATCH = 2
SEQ = 8192
DEPTH = 2

N_MIXERS = 2
N_CONV_LAYERS = (DEPTH + 1) // 2
N_GLA_LAYERS = DEPTH // 2
EPS = 1e-6
CONV_WIDTH = 31
GLA_HEADS = 4
GLA_DK = D_MODEL // 2
GLA_DV = D_MODEL
GLA_DK_HEAD = GLA_DK // GLA_HEADS
GLA_DV_HEAD = GLA_DV // GLA_HEADS
GLA_GATE_RANK = 16
GLA_GATE_NORMALIZER = 16.0
GLA_CHUNK = 64
D_FF = int(math.ceil(8 * D_MODEL / 3 / 256) * 256)

kernel_name = "hybrid_conformer_conv_gla_swiglu"


def rms_norm(x, g):
    xf = x.astype(jnp.float32)
    y = xf * lax.rsqrt(jnp.mean(xf * xf, axis=-1, keepdims=True) + EPS)
    return (y * g.astype(jnp.float32)).astype(x.dtype)


def layer_norm(x, g, b):
    xf = x.astype(jnp.float32)
    mu = jnp.mean(xf, axis=-1, keepdims=True)
    var = jnp.mean(jnp.square(xf - mu), axis=-1, keepdims=True)
    y = (xf - mu) * lax.rsqrt(var + EPS)
    return (y * g.astype(jnp.float32) + b.astype(jnp.float32)).astype(x.dtype)


def conformer_conv_module(h, w_in, b_in, w_dw, b_dw, ln_g, ln_b, w_out, b_out):
    u = h @ w_in + b_in
    a, gate = jnp.split(u, 2, axis=-1)
    u = a * jax.nn.sigmoid(gate)
    u = lax.conv_general_dilated(
        u, w_dw[:, None, :], window_strides=(1,), padding=[(CONV_WIDTH - 1, 0)],
        dimension_numbers=("NWC", "WIO", "NWC"), feature_group_count=D_MODEL) + b_dw
    u = jax.nn.silu(layer_norm(u, ln_g, ln_b))
    return u @ w_out + b_out


def _to_chunks(t):
    b, s, hh, d = t.shape
    return t.reshape(b, s // GLA_CHUNK, GLA_CHUNK, hh, d).transpose(1, 0, 3, 2, 4)


def chunked_gla(q, k, v, log_a):
    b, s, hh, dk = q.shape
    dv = v.shape[-1]
    qc, kc, vc, ac = (_to_chunks(t) for t in (q, k, v, log_a))
    causal = jnp.tril(jnp.ones((GLA_CHUNK, GLA_CHUNK), dtype=bool))[:, :, None]

    def step(state, inp):
        qi, ki, vi, ai = inp
        cum = jnp.cumsum(ai, axis=2)
        o_inter = jnp.einsum("bhcd,bhde->bhce", qi * jnp.exp(cum), state)
        diff = cum[:, :, :, None, :] - cum[:, :, None, :, :]
        decay = jnp.where(causal, jnp.exp(jnp.where(causal, diff, 0.0)), 0.0)
        scores = jnp.einsum("bhid,bhjd,bhijd->bhij", qi, ki, decay)
        o_intra = jnp.einsum("bhij,bhje->bhie", scores, vi)
        last = cum[:, :, -1:, :]
        new_state = jnp.exp(last[:, :, 0, :])[..., None] * state + jnp.einsum(
            "bhcd,bhce->bhde", ki * jnp.exp(last - cum), vi)
        return new_state, o_inter + o_intra

    state0 = jnp.zeros((b, hh, dk, dv), jnp.float32)
    _, oc = lax.scan(step, state0, (qc, kc, vc, ac))
    return oc.transpose(1, 0, 3, 2, 4).reshape(b, s, hh, dv)


def gla_mixer(h, w_in, gate_w1, gate_w2, gate_b, head_norm_g, w_out):
    b, s, _ = h.shape
    proj = h @ w_in
    q, k, v, r = jnp.split(proj, [GLA_DK, 2 * GLA_DK, 2 * GLA_DK + GLA_DV], axis=-1)
    log_a = jax.nn.log_sigmoid(((h @ gate_w1) @ gate_w2 + gate_b).astype(jnp.float32)) / GLA_GATE_NORMALIZER
    q = q.astype(jnp.float32).reshape(b, s, GLA_HEADS, GLA_DK_HEAD) * (GLA_DK_HEAD ** -0.5)
    k = k.astype(jnp.float32).reshape(b, s, GLA_HEADS, GLA_DK_HEAD)
    v = v.astype(jnp.float32).reshape(b, s, GLA_HEADS, GLA_DV_HEAD)
    log_a = log_a.reshape(b, s, GLA_HEADS, GLA_DK_HEAD)
    o = chunked_gla(q, k, v, log_a)
    o = o * lax.rsqrt(jnp.mean(o * o, axis=-1, keepdims=True) + EPS) * head_norm_g.astype(jnp.float32)
    o = o.reshape(b, s, GLA_DV).astype(h.dtype) * jax.nn.silu(r)
    return o @ w_out


def swiglu_ffn(h, w_gate_up, w_down):
    a, g = jnp.split(h @ w_gate_up, 2, axis=-1)
    return (jax.nn.silu(a) * g) @ w_down


def setup_inputs(seed: int = 0) -> dict:
    key = jax.random.key(seed)
    ks = jax.random.split(key, 24)
    f32 = jnp.float32
    D = D_MODEL

    def nrm(k, shape, scale):
        return jax.random.normal(k, shape, f32) * scale

    def gain(k, shape):
        return 1.0 + 0.02 * jax.random.normal(k, shape, f32)

    return {
        "x": jax.random.normal(ks[0], (BATCH, SEQ, D), f32),
        "norm_mix_g": gain(ks[1], (DEPTH, D)),
        "norm_ffn_g": gain(ks[2], (DEPTH, D)),
        "conv_w_in": nrm(ks[3], (N_CONV_LAYERS, D, 2 * D), D ** -0.5),
        "conv_b_in": nrm(ks[4], (N_CONV_LAYERS, 2 * D), 0.02),
        "conv_w_dw": nrm(ks[5], (N_CONV_LAYERS, CONV_WIDTH, D), CONV_WIDTH ** -0.5),
        "conv_b_dw": nrm(ks[6], (N_CONV_LAYERS, D), 0.02),
        "conv_ln_g": gain(ks[7], (N_CONV_LAYERS, D)),
        "conv_ln_b": nrm(ks[8], (N_CONV_LAYERS, D), 0.02),
        "conv_w_out": nrm(ks[9], (N_CONV_LAYERS, D, D), D ** -0.5),
        "conv_b_out": nrm(ks[10], (N_CONV_LAYERS, D), 0.02),
        "gla_w_in": nrm(ks[11], (N_GLA_LAYERS, D, 2 * GLA_DK + 2 * GLA_DV), D ** -0.5),
        "gla_gate_w1": nrm(ks[12], (N_GLA_LAYERS, D, GLA_GATE_RANK), D ** -0.5),
        "gla_gate_w2": nrm(ks[13], (N_GLA_LAYERS, GLA_GATE_RANK, GLA_DK), GLA_GATE_RANK ** -0.5),
        "gla_gate_b": nrm(ks[14], (N_GLA_LAYERS, GLA_DK), 0.1),
        "gla_head_norm_g": gain(ks[15], (N_GLA_LAYERS, GLA_HEADS, GLA_DV_HEAD)),
        "gla_w_out": nrm(ks[16], (N_GLA_LAYERS, GLA_DV, D), GLA_DV ** -0.5),
        "ffn_w_gate_up": nrm(ks[17], (DEPTH, D, 2 * D_FF), D ** -0.5),
        "ffn_w_down": nrm(ks[18], (DEPTH, D_FF, D), D_FF ** -0.5),
        "final_norm_g": gain(ks[19], (D,)),
    }


def reference(x, norm_mix_g, norm_ffn_g, conv_w_in, conv_b_in, conv_w_dw, conv_b_dw, conv_ln_g, conv_ln_b,
              conv_w_out, conv_b_out, gla_w_in, gla_gate_w1, gla_gate_w2, gla_gate_b, gla_head_norm_g,
              gla_w_out, ffn_w_gate_up, ffn_w_down, final_norm_g):
    for i in range(DEPTH):
        j = i // N_MIXERS
        h = rms_norm(x, norm_mix_g[i])
        if i % N_MIXERS == 0:
            h = conformer_conv_module(h, conv_w_in[j], conv_b_in[j], conv_w_dw[j], conv_b_dw[j],
                                      conv_ln_g[j], conv_ln_b[j], conv_w_out[j], conv_b_out[j])
        else:
            h = gla_mixer(h, gla_w_in[j], gla_gate_w1[j], gla_gate_w2[j], gla_gate_b[j],
                          gla_head_norm_g[j], gla_w_out[j])
        x = x + h
        x = x + swiglu_ffn(rms_norm(x, norm_ffn_g[i]), ffn_w_gate_up[i], ffn_w_down[i])
    return rms_norm(x, final_norm_g)
```

```python
import functools
import math

import jax
import jax.numpy as jnp
import numpy as np
from jax import lax
from jax.experimental import pallas as pl
from jax.experimental.pallas import tpu as pltpu

F32 = jnp.float32
BF16 = jnp.bfloat16

EPS = 1e-6
CONV_WIDTH = 31
GLA_HEADS = 4
GLA_GATE_NORMALIZER = 16.0

V7X_LANES = 128
V7X_SUBLANES = 8
V7X_VMEM_BYTES = 64 * 1024 * 1024

VMEM_LIMIT = V7X_VMEM_BYTES - 8 * 1024 * 1024
TM_PROJ = 512
TN_PROJ = 512
TM_CONV = 256
CONV_HALO = 32
CONV_ROWS = 128
GLA_CHUNK = 64
GLA_LT = 512
GLA_LEVELS = (32, 16, 8, 4, 2, 1)


def _params(*sem):
    return pltpu.CompilerParams(dimension_semantics=sem, vmem_limit_bytes=VMEM_LIMIT)


def _rms_norm_bf16(x, g):
    ms = jnp.mean(x * x, axis=-1, keepdims=True)
    return (x * lax.rsqrt(ms + EPS) * g).astype(BF16)


def _sigmoid(x):
    return 1.0 / (1.0 + jnp.exp(-x))


def _conv_in_kernel(x_ref, g_ref, wa_ref, wg_ref, ba_ref, bg_ref, u_ref, h_scr):
    @pl.when(pl.program_id(1) == 0)
    def _():
        h_scr[...] = _rms_norm_bf16(x_ref[...], g_ref[...])

    h = h_scr[...]
    a = jnp.dot(h, wa_ref[...], preferred_element_type=F32) + ba_ref[...]
    g = jnp.dot(h, wg_ref[...], preferred_element_type=F32) + bg_ref[...]
    u_ref[...] = a * _sigmoid(g)


def _conv_in(x, g, w_in, b_in):
    t, d = x.shape
    tm, tn = TM_PROJ, TN_PROJ
    nj = d // tn
    return pl.pallas_call(
        _conv_in_kernel,
        grid=(t // tm, nj),
        in_specs=[
            pl.BlockSpec((tm, d), lambda i, j: (i, 0)),
            pl.BlockSpec((1, d), lambda i, j: (0, 0)),
            pl.BlockSpec((d, tn), lambda i, j: (0, j)),
            pl.BlockSpec((d, tn), lambda i, j: (0, j + nj)),
            pl.BlockSpec((1, tn), lambda i, j: (0, j)),
            pl.BlockSpec((1, tn), lambda i, j: (0, j + nj)),
        ],
        out_specs=pl.BlockSpec((tm, tn), lambda i, j: (i, j)),
        out_shape=jax.ShapeDtypeStruct((t, d), F32),
        scratch_shapes=[pltpu.VMEM((tm, d), BF16)],
        compiler_params=_params("parallel", "arbitrary"),
        name="conv_in_glu",
    )(x, g, w_in, w_in, b_in, b_in)


def _conv_mix_kernel(u_ref, halo_ref, x_ref, wdw_ref, bdw_ref, lng_ref, lnb_ref, wo_ref, bo_ref,
                     o_ref, ext_scr, c_scr, *, tiles_per_seq):
    tm, d = u_ref.shape
    first = (pl.program_id(0) % tiles_per_seq) == 0
    ext_scr[0:CONV_HALO, :] = jnp.where(first, 0.0, halo_ref[...])
    ext_scr[CONV_HALO:, :] = u_ref[...]
    shift = CONV_HALO - (CONV_WIDTH - 1)

    def lane_block(c, carry):
        col = pl.multiple_of(c * V7X_LANES, V7X_LANES)
        lanes = pl.ds(col, V7X_LANES)
        for r0 in range(0, tm, CONV_ROWS):
            acc = None
            for k in range(CONV_WIDTH):
                term = ext_scr[pl.ds(r0 + shift + k, CONV_ROWS), lanes] * wdw_ref[pl.ds(k, 1), lanes]
                acc = term if acc is None else acc + term
            c_scr[pl.ds(r0, CONV_ROWS), lanes] = acc + bdw_ref[:, lanes]
        return carry

    lax.fori_loop(0, d // V7X_LANES, lane_block, 0)

    c = c_scr[...]
    mu = jnp.mean(c, axis=-1, keepdims=True)
    cc = c - mu
    var = jnp.mean(cc * cc, axis=-1, keepdims=True)
    y = cc * lax.rsqrt(var + EPS) * lng_ref[...] + lnb_ref[...]
    s = (y * _sigmoid(y)).astype(BF16)
    o_ref[...] = x_ref[...] + jnp.dot(s, wo_ref[...], preferred_element_type=F32) + bo_ref[...]


def _conv_mix(u, x, w_dw, b_dw, ln_g, ln_b, w_out, b_out, seq):
    t, d = x.shape
    tm = TM_CONV
    halo_blocks = tm // CONV_HALO
    row = lambda i: (i, 0)
    fixed = lambda i: (0, 0)
    return pl.pallas_call(
        functools.partial(_conv_mix_kernel, tiles_per_seq=seq // tm),
        grid=(t // tm,),
        in_specs=[
            pl.BlockSpec((tm, d), row),
            pl.BlockSpec((CONV_HALO, d), lambda i: (jnp.maximum(i * halo_blocks - 1, 0), 0)),
            pl.BlockSpec((tm, d), row),
            pl.BlockSpec((CONV_WIDTH, d), fixed),
            pl.BlockSpec((1, d), fixed),
            pl.BlockSpec((1, d), fixed),
            pl.BlockSpec((1, d), fixed),
            pl.BlockSpec((d, d), fixed),
            pl.BlockSpec((1, d), fixed),
        ],
        out_specs=pl.BlockSpec((tm, d), row),
        out_shape=jax.ShapeDtypeStruct((t, d), F32),
        scratch_shapes=[pltpu.VMEM((tm + CONV_HALO, d), F32), pltpu.VMEM((tm, d), F32)],
        compiler_params=_params("parallel"),
        name="conv_dw_ln_out",
    )(u, u, x, w_dw, b_dw, ln_g, ln_b, w_out, b_out)


def _ffn_kernel(x_ref, g_ref, wa_ref, wg_ref, wd_ref, fg_ref, o_ref, h_scr, *, final_norm):
    j = pl.program_id(1)

    @pl.when(j == 0)
    def _():
        h_scr[...] = _rms_norm_bf16(x_ref[...], g_ref[...])

    h = h_scr[...]
    a = jnp.dot(h, wa_ref[...], preferred_element_type=F32)
    g = jnp.dot(h, wg_ref[...], preferred_element_type=F32)
    p = (a * _sigmoid(a) * g).astype(BF16)
    part = jnp.dot(p, wd_ref[...], preferred_element_type=F32)

    @pl.when(j == 0)
    def _():
        o_ref[...] = x_ref[...] + part

    @pl.when(j > 0)
    def _():
        o_ref[...] += part

    if final_norm:
        @pl.when(j == pl.num_programs(1) - 1)
        def _():
            y = o_ref[...]
            ms = jnp.mean(y * y, axis=-1, keepdims=True)
            o_ref[...] = y * lax.rsqrt(ms + EPS) * fg_ref[...]


def _ffn(x, g, w_gate_up, w_down, final_g, final_norm):
    t, d = x.shape
    d_ff = w_down.shape[0]
    tm, tf = TM_PROJ, TN_PROJ
    nj = d_ff // tf
    return pl.pallas_call(
        functools.partial(_ffn_kernel, final_norm=final_norm),
        grid=(t // tm, nj),
        in_specs=[
            pl.BlockSpec((tm, d), lambda i, j: (i, 0)),
            pl.BlockSpec((1, d), lambda i, j: (0, 0)),
            pl.BlockSpec((d, tf), lambda i, j: (0, j)),
            pl.BlockSpec((d, tf), lambda i, j: (0, j + nj)),
            pl.BlockSpec((tf, d), lambda i, j: (j, 0)),
            pl.BlockSpec((1, d), lambda i, j: (0, 0)),
        ],
        out_specs=pl.BlockSpec((tm, d), lambda i, j: (i, 0)),
        out_shape=jax.ShapeDtypeStruct((t, d), F32),
        scratch_shapes=[pltpu.VMEM((tm, d), BF16)],
        compiler_params=_params("parallel", "arbitrary"),
        name="ffn_swiglu",
    )(x, g, w_gate_up, w_gate_up, w_down, final_g)


def _gla_proj_kernel(x_ref, g_ref, w_ref, w1_ref, w2_ref, gb_ref, proj_ref, la_ref, h_scr):
    @pl.when(pl.program_id(1) == 0)
    def _():
        h = _rms_norm_bf16(x_ref[...], g_ref[...])
        h_scr[...] = h
        low = jnp.dot(h, w1_ref[...], preferred_element_type=F32).astype(BF16)
        z = jnp.dot(low, w2_ref[...], preferred_element_type=F32) + gb_ref[...]
        log_sig = jnp.minimum(z, 0.0) - jnp.log(1.0 + jnp.exp(-jnp.abs(z)))
        la_ref[...] = log_sig * (1.0 / GLA_GATE_NORMALIZER)

    proj_ref[...] = jnp.dot(h_scr[...], w_ref[...], preferred_element_type=F32)


def _gla_proj(x, g, w_in, w1, w2, gate_b):
    t, d = x.shape
    n = w_in.shape[1]
    dk = w2.shape[1]
    rank = w1.shape[1]
    tm, tn = TM_PROJ, TN_PROJ
    return pl.pallas_call(
        _gla_proj_kernel,
        grid=(t // tm, n // tn),
        in_specs=[
            pl.BlockSpec((tm, d), lambda i, j: (i, 0)),
            pl.BlockSpec((1, d), lambda i, j: (0, 0)),
            pl.BlockSpec((d, tn), lambda i, j: (0, j)),
            pl.BlockSpec((d, rank), lambda i, j: (0, 0)),
            pl.BlockSpec((rank, dk), lambda i, j: (0, 0)),
            pl.BlockSpec((1, dk), lambda i, j: (0, 0)),
        ],
        out_specs=[
            pl.BlockSpec((tm, tn), lambda i, j: (i, j)),
            pl.BlockSpec((tm, dk), lambda i, j: (i, 0)),
        ],
        out_shape=[jax.ShapeDtypeStruct((t, n), F32), jax.ShapeDtypeStruct((t, dk), F32)],
        scratch_shapes=[pltpu.VMEM((tm, d), BF16)],
        compiler_params=_params("parallel", "arbitrary"),
        name="gla_in_proj",
    )(x, g, w_in, w1, w2, gate_b)


def _gla_decay_matrix():
    c = GLA_CHUNK
    i = np.arange(c)[:, None]
    t = np.arange(c)[None, :]
    blocks = [t <= i, t > i]
    for b in GLA_LEVELS:
        ref = (i // (2 * b)) * (2 * b) + b - 1
        blocks.append(np.where(i > ref, (t > ref) & (t <= i), (t > i) & (t <= ref)))
    return np.concatenate(blocks, axis=0).astype(np.float32)


def _gla_kernel(q_ref, k_ref, v_ref, r_ref, la_ref, amat_ref, hg_ref, o_ref, s_scr):
    c = GLA_CHUNK
    lt, dk = q_ref.shape
    dv = v_ref.shape[1]

    @pl.when(pl.program_id(2) == 0)
    def _():
        s_scr[...] = jnp.zeros_like(s_scr)

    ri = lax.broadcasted_iota(jnp.int32, (c, c), 0)
    ci = lax.broadcasted_iota(jnp.int32, (c, c), 1)
    masks = [ri == ci]
    for b in GLA_LEVELS:
        masks.append((ri // (2 * b) == ci // (2 * b)) & ((ri // b) % 2 == 1) & ((ci // b) % 2 == 0))
    amat = amat_ref[...]
    head_g = hg_ref[0]
    nt = (((1,), (1,)), ((), ()))

    def chunk(ic, carry):
        rows = pl.ds(pl.multiple_of(ic * c, c), c)
        q = q_ref[rows, :] * (dk ** -0.5)
        k = k_ref[rows, :]
        v = v_ref[rows, :].astype(BF16)
        la = la_ref[rows, :]
        la_hi = la.astype(BF16)
        la_lo = (la - la_hi.astype(F32)).astype(BF16)
        expo = (jnp.dot(amat, la_hi, preferred_element_type=F32)
                + jnp.dot(amat, la_lo, preferred_element_type=F32))
        e = jnp.exp(expo)
        state = s_scr[...]
        o = jnp.dot((q * e[0:c]).astype(BF16), state.astype(BF16), preferred_element_type=F32)

        scores = jnp.where(masks[0], lax.dot_general(q.astype(BF16), k.astype(BF16), nt,
                                                     preferred_element_type=F32), 0.0)
        for lvl in range(len(GLA_LEVELS)):
            eb = e[(2 + lvl) * c:(3 + lvl) * c]
            sb = lax.dot_general((q * eb).astype(BF16), (k * eb).astype(BF16), nt,
                                 preferred_element_type=F32)
            scores = scores + jnp.where(masks[1 + lvl], sb, 0.0)
        o = o + jnp.dot(scores.astype(BF16), v, preferred_element_type=F32)

        k_rem_t = (k * e[c:2 * c]).T.astype(BF16)
        e_last = jnp.exp(jnp.sum(la.T, axis=1, keepdims=True))
        s_scr[...] = state * e_last + jnp.dot(k_rem_t, v, preferred_element_type=F32)

        ms = jnp.mean(o * o, axis=-1, keepdims=True)
        r = r_ref[rows, :]
        o_ref[rows, :] = (o * lax.rsqrt(ms + EPS) * head_g * (r * _sigmoid(r))).astype(BF16)
        return carry

    lax.fori_loop(0, lt // c, chunk, 0)


def _gla(proj, log_a, head_g, batch, seq):
    t = proj.shape[0]
    dk_head = log_a.shape[1] // GLA_HEADS
    dv_total = (proj.shape[1] - 2 * log_a.shape[1]) // 2
    dv_head = dv_total // GLA_HEADS
    lt = GLA_LT
    steps = seq // lt
    k_off = log_a.shape[1] // dk_head
    v_off = 2 * log_a.shape[1] // dv_head
    r_off = v_off + dv_total // dv_head
    amat = jnp.asarray(_gla_decay_matrix(), BF16)
    row = lambda b, h, s: b * steps + s
    return pl.pallas_call(
        _gla_kernel,
        grid=(batch, GLA_HEADS, steps),
        in_specs=[
            pl.BlockSpec((lt, dk_head), lambda b, h, s: (row(b, h, s), h)),
            pl.BlockSpec((lt, dk_head), lambda b, h, s: (row(b, h, s), k_off + h)),
            pl.BlockSpec((lt, dv_head), lambda b, h, s: (row(b, h, s), v_off + h)),
            pl.BlockSpec((lt, dv_head), lambda b, h, s: (row(b, h, s), r_off + h)),
            pl.BlockSpec((lt, dk_head), lambda b, h, s: (row(b, h, s), h)),
            pl.BlockSpec(amat.shape, lambda b, h, s: (0, 0)),
            pl.BlockSpec((1, 1, dv_head), lambda b, h, s: (h, 0, 0)),
        ],
        out_specs=pl.BlockSpec((lt, dv_head), lambda b, h, s: (row(b, h, s), h)),
        out_shape=jax.ShapeDtypeStruct((t, dv_total), BF16),
        scratch_shapes=[pltpu.VMEM((dk_head, dv_head), F32)],
        compiler_params=_params("parallel", "parallel", "arbitrary"),
        name="gla_chunked",
    )(proj, proj, proj, proj, log_a, amat, head_g.reshape(GLA_HEADS, 1, dv_head))


def _out_proj_kernel(a_ref, w_ref, x_ref, o_ref):
    o_ref[...] = x_ref[...] + jnp.dot(a_ref[...], w_ref[...], preferred_element_type=F32)


def _out_proj(a, w, x):
    t, d = x.shape
    kdim = a.shape[1]
    tm = TM_PROJ
    return pl.pallas_call(
        _out_proj_kernel,
        grid=(t // tm,),
        in_specs=[
            pl.BlockSpec((tm, kdim), lambda i: (i, 0)),
            pl.BlockSpec((kdim, d), lambda i: (0, 0)),
            pl.BlockSpec((tm, d), lambda i: (i, 0)),
        ],
        out_specs=pl.BlockSpec((tm, d), lambda i: (i, 0)),
        out_shape=jax.ShapeDtypeStruct((t, d), F32),
        compiler_params=_params("parallel"),
        name="gla_out_proj",
    )(a, w, x)


def kernel(x, norm_mix_g, norm_ffn_g, conv_w_in, conv_b_in, conv_w_dw, conv_b_dw, conv_ln_g, conv_ln_b,
           conv_w_out, conv_b_out, gla_w_in, gla_gate_w1, gla_gate_w2, gla_gate_b, gla_head_norm_g,
           gla_w_out, ffn_w_gate_up, ffn_w_down, final_norm_g):
    batch, seq, d = x.shape
    depth = norm_mix_g.shape[0]
    assert seq % TM_PROJ == 0 and seq % TM_CONV == 0 and seq % GLA_LT == 0
    row2d = lambda a: a.reshape(1, -1)
    h = x.reshape(batch * seq, d)
    for i in range(depth):
        j = i // 2
        g_mix = row2d(norm_mix_g[i])
        if i % 2 == 0:
            u = _conv_in(h, g_mix, conv_w_in[j].astype(BF16), row2d(conv_b_in[j]))
            h = _conv_mix(u, h, conv_w_dw[j], row2d(conv_b_dw[j]), row2d(conv_ln_g[j]),
                          row2d(conv_ln_b[j]), conv_w_out[j].astype(BF16), row2d(conv_b_out[j]), seq)
        else:
            proj, log_a = _gla_proj(h, g_mix, gla_w_in[j].astype(BF16), gla_gate_w1[j].astype(BF16),
                                    gla_gate_w2[j].astype(BF16), row2d(gla_gate_b[j]))
            a = _gla(proj, log_a, gla_head_norm_g[j], batch, seq)
            h = _out_proj(a, gla_w_out[j].astype(BF16), h)
        h = _ffn(h, row2d(norm_ffn_g[i]), ffn_w_gate_up[i].astype(BF16), ffn_w_down[i].astype(BF16),
                 row2d(final_norm_g), final_norm=(i == depth - 1))
    return h.reshape(batch, seq, d)
```

```python
import functools

import jax
import jax.numpy as jnp
import numpy as np
from jax import lax
from jax.experimental import pallas as pl
from jax.experimental.pallas import tpu as pltpu

F32 = jnp.float32
BF16 = jnp.bfloat16

EPS = 1e-6
CONV_WIDTH = 31
GLA_HEADS = 4
GLA_GATE_NORMALIZER = 16.0

V7X_LANES = 128
V7X_SUBLANES = 8
V7X_VMEM_BYTES = 64 * 1024 * 1024

VMEM_LIMIT = V7X_VMEM_BYTES - 8 * 1024 * 1024
TM_PROJ = 1024
TM_FFN = 512
TN_PROJ = 512
TM_CONV = 256
CONV_HALO = 32
CONV_ROWS = 128
GLA_CHUNK = 64
GLA_LT = 512
GLA_LEVELS = (32, 16, 8, 4, 2, 1)


def _params(*sem):
    return pltpu.CompilerParams(dimension_semantics=sem, vmem_limit_bytes=VMEM_LIMIT)


def _rms_norm_bf16(x, g):
    ms = jnp.mean(x * x, axis=-1, keepdims=True)
    return (x * lax.rsqrt(ms + EPS) * g).astype(BF16)


def _sigmoid(x):
    return 1.0 / (1.0 + jnp.exp(-x))


def _conv_in_kernel(x_ref, g_ref, wa_ref, wg_ref, ba_ref, bg_ref, u_ref, h_scr):
    @pl.when(pl.program_id(1) == 0)
    def _():
        h_scr[...] = _rms_norm_bf16(x_ref[...], g_ref[...])

    h = h_scr[...]
    a = jnp.dot(h, wa_ref[...], preferred_element_type=F32) + ba_ref[...]
    g = jnp.dot(h, wg_ref[...], preferred_element_type=F32) + bg_ref[...]
    u_ref[...] = a * _sigmoid(g)


def _conv_in(x, g, w_in, b_in, layer):
    t, d = x.shape
    tm, tn = TM_PROJ, TN_PROJ
    nj = d // tn
    return pl.pallas_call(
        _conv_in_kernel,
        grid=(t // tm, nj),
        in_specs=[
            pl.BlockSpec((tm, d), lambda i, j: (i, 0)),
            pl.BlockSpec((1, d), lambda i, j: (0, 0)),
            pl.BlockSpec((None, d, tn), lambda i, j: (layer, 0, j)),
            pl.BlockSpec((None, d, tn), lambda i, j: (layer, 0, j + nj)),
            pl.BlockSpec((None, 1, tn), lambda i, j: (layer, 0, j)),
            pl.BlockSpec((None, 1, tn), lambda i, j: (layer, 0, j + nj)),
        ],
        out_specs=pl.BlockSpec((tm, tn), lambda i, j: (i, j)),
        out_shape=jax.ShapeDtypeStruct((t, d), F32),
        scratch_shapes=[pltpu.VMEM((tm, d), BF16)],
        compiler_params=_params("parallel", "arbitrary"),
        name="conv_in_glu",
    )(x, g, w_in, w_in, b_in, b_in)


def _conv_mix_kernel(u_ref, halo_ref, x_ref, wdw_ref, bdw_ref, lng_ref, lnb_ref, wo_ref, bo_ref,
                     o_ref, ext_scr, sh_scr, c_scr, *, tiles_per_seq):
    tm, d = u_ref.shape
    sub = V7X_SUBLANES
    first = (pl.program_id(0) % tiles_per_seq) == 0
    ext_scr[0:CONV_HALO, :] = jnp.where(first, 0.0, halo_ref[...])
    ext_scr[CONV_HALO:, :] = u_ref[...]
    shift = CONV_HALO - (CONV_WIDTH - 1)
    groups = (tm + CONV_HALO) // sub
    sub_idx = lax.broadcasted_iota(jnp.int32, (groups - 1, sub, V7X_LANES), 1)

    def lane_block(c, carry):
        lanes = pl.ds(pl.multiple_of(c * V7X_LANES, V7X_LANES), V7X_LANES)
        x3 = ext_scr[:, lanes].reshape(groups, sub, V7X_LANES)
        for r in range(1, sub):
            rot = pltpu.roll(x3, sub - r, axis=1)
            y = jnp.where(sub_idx < sub - r, rot[:-1], rot[1:])
            sh_scr[r, 0:(groups - 1) * sub, :] = y.reshape((groups - 1) * sub, V7X_LANES)
        for r0 in range(0, tm, CONV_ROWS):
            acc = None
            for k in range(CONV_WIDTH):
                a, r = divmod(shift + k, sub)
                rows = pl.ds(r0 + a * sub, CONV_ROWS)
                seg = ext_scr[rows, lanes] if r == 0 else sh_scr[r, rows, :]
                term = seg * wdw_ref[pl.ds(k, 1), lanes]
                acc = term if acc is None else acc + term
            c_scr[pl.ds(r0, CONV_ROWS), lanes] = acc + bdw_ref[:, lanes]
        return carry

    lax.fori_loop(0, d // V7X_LANES, lane_block, 0)

    c = c_scr[...]
    mu = jnp.mean(c, axis=-1, keepdims=True)
    cc = c - mu
    var = jnp.mean(cc * cc, axis=-1, keepdims=True)
    y = cc * lax.rsqrt(var + EPS) * lng_ref[...] + lnb_ref[...]
    s = (y * _sigmoid(y)).astype(BF16)
    o_ref[...] = x_ref[...] + jnp.dot(s, wo_ref[...], preferred_element_type=F32) + bo_ref[...]


def _conv_mix(u, x, w_dw, b_dw, ln_g, ln_b, w_out, b_out, layer, seq):
    t, d = x.shape
    tm = TM_CONV
    halo_blocks = tm // CONV_HALO
    row = lambda i: (i, 0)
    fixed = lambda i: (layer, 0, 0)
    return pl.pallas_call(
        functools.partial(_conv_mix_kernel, tiles_per_seq=seq // tm),
        grid=(t // tm,),
        in_specs=[
            pl.BlockSpec((tm, d), row),
            pl.BlockSpec((CONV_HALO, d), lambda i: (jnp.maximum(i * halo_blocks - 1, 0), 0)),
            pl.BlockSpec((tm, d), row),
            pl.BlockSpec((None, CONV_WIDTH, d), fixed),
            pl.BlockSpec((None, 1, d), fixed),
            pl.BlockSpec((None, 1, d), fixed),
            pl.BlockSpec((None, 1, d), fixed),
            pl.BlockSpec((None, d, d), fixed),
            pl.BlockSpec((None, 1, d), fixed),
        ],
        out_specs=pl.BlockSpec((tm, d), row),
        out_shape=jax.ShapeDtypeStruct((t, d), F32),
        scratch_shapes=[
            pltpu.VMEM((tm + CONV_HALO, d), F32),
            pltpu.VMEM((V7X_SUBLANES, tm + CONV_HALO, V7X_LANES), F32),
            pltpu.VMEM((tm, d), F32),
        ],
        compiler_params=_params("parallel"),
        name="conv_dw_ln_out",
    )(u, u, x, w_dw, b_dw, ln_g, ln_b, w_out, b_out)


def _ffn_kernel(x_ref, g_ref, wa_ref, wg_ref, wd_ref, fg_ref, o_ref, h_scr, *, final_norm):
    j = pl.program_id(1)

    @pl.when(j == 0)
    def _():
        x = x_ref[...]
        h_scr[...] = _rms_norm_bf16(x, g_ref[...])
        o_ref[...] = x

    h = h_scr[...]
    a = jnp.dot(h, wa_ref[...], preferred_element_type=F32)
    g = jnp.dot(h, wg_ref[...], preferred_element_type=F32)
    p = (a * _sigmoid(a) * g).astype(BF16)
    o_ref[...] += jnp.dot(p, wd_ref[...], preferred_element_type=F32)

    if final_norm:
        @pl.when(j == pl.num_programs(1) - 1)
        def _():
            y = o_ref[...]
            ms = jnp.mean(y * y, axis=-1, keepdims=True)
            o_ref[...] = y * lax.rsqrt(ms + EPS) * fg_ref[...]


def _ffn(x, g, w_gate_up, w_down, final_g, layer, final_norm):
    t, d = x.shape
    d_ff = w_down.shape[1]
    tm, tf = TM_FFN, TN_PROJ
    nj = d_ff // tf
    return pl.pallas_call(
        functools.partial(_ffn_kernel, final_norm=final_norm),
        grid=(t // tm, nj),
        in_specs=[
            pl.BlockSpec((tm, d), lambda i, j: (i, 0)),
            pl.BlockSpec((None, 1, d), lambda i, j: (layer, 0, 0)),
            pl.BlockSpec((None, d, tf), lambda i, j: (layer, 0, j)),
            pl.BlockSpec((None, d, tf), lambda i, j: (layer, 0, j + nj)),
            pl.BlockSpec((None, tf, d), lambda i, j: (layer, j, 0)),
            pl.BlockSpec((1, d), lambda i, j: (0, 0)),
        ],
        out_specs=pl.BlockSpec((tm, d), lambda i, j: (i, 0)),
        out_shape=jax.ShapeDtypeStruct((t, d), F32),
        scratch_shapes=[pltpu.VMEM((tm, d), BF16)],
        compiler_params=_params("parallel", "arbitrary"),
        name="ffn_swiglu",
    )(x, g, w_gate_up, w_gate_up, w_down, final_g)


def _gla_proj_kernel(x_ref, g_ref, w_ref, w1_ref, w2_ref, gb_ref, proj_ref, la_ref, h_scr):
    @pl.when(pl.program_id(1) == 0)
    def _():
        h = _rms_norm_bf16(x_ref[...], g_ref[...])
        h_scr[...] = h
        low = jnp.dot(h, w1_ref[...], preferred_element_type=F32).astype(BF16)
        z = jnp.dot(low, w2_ref[...], preferred_element_type=F32) + gb_ref[...]
        log_sig = jnp.minimum(z, 0.0) - jnp.log(1.0 + jnp.exp(-jnp.abs(z)))
        la_ref[...] = log_sig * (1.0 / GLA_GATE_NORMALIZER)

    proj_ref[...] = jnp.dot(h_scr[...], w_ref[...], preferred_element_type=F32).astype(BF16)


def _gla_proj(x, g, w_in, w1, w2, gate_b, layer):
    t, d = x.shape
    n = w_in.shape[2]
    dk = w2.shape[2]
    rank = w1.shape[2]
    tm, tn = TM_PROJ, TN_PROJ
    fixed = lambda i, j: (layer, 0, 0)
    return pl.pallas_call(
        _gla_proj_kernel,
        grid=(t // tm, n // tn),
        in_specs=[
            pl.BlockSpec((tm, d), lambda i, j: (i, 0)),
            pl.BlockSpec((1, d), lambda i, j: (0, 0)),
            pl.BlockSpec((None, d, tn), lambda i, j: (layer, 0, j)),
            pl.BlockSpec((None, d, rank), fixed),
            pl.BlockSpec((None, rank, dk), fixed),
            pl.BlockSpec((None, 1, dk), fixed),
        ],
        out_specs=[
            pl.BlockSpec((tm, tn), lambda i, j: (i, j)),
            pl.BlockSpec((tm, dk), lambda i, j: (i, 0)),
        ],
        out_shape=[jax.ShapeDtypeStruct((t, n), BF16), jax.ShapeDtypeStruct((t, dk), F32)],
        scratch_shapes=[pltpu.VMEM((tm, d), BF16)],
        compiler_params=_params("parallel", "arbitrary"),
        name="gla_in_proj",
    )(x, g, w_in, w1, w2, gate_b)


def _gla_decay_matrix():
    c = GLA_CHUNK
    i = np.arange(c)[:, None]
    t = np.arange(c)[None, :]
    blocks = [t <= i, t > i]
    for b in GLA_LEVELS:
        ref = (i // (2 * b)) * (2 * b) + b - 1
        blocks.append(np.where(i > ref, (t > ref) & (t <= i), (t > i) & (t <= ref)))
    a = np.concatenate(blocks, axis=0).astype(np.float32)
    return np.concatenate([a, a], axis=1)


def _gla_kernel(qk_ref, v_ref, r_ref, la_ref, amat_ref, hg_ref, o_ref, s_scr):
    c = GLA_CHUNK
    lt = qk_ref.shape[0]
    heads = s_scr.shape[0]
    dk, dv = s_scr.shape[1], s_scr.shape[2]
    k_base = heads * dk

    @pl.when(pl.program_id(1) == 0)
    def _():
        s_scr[...] = jnp.zeros_like(s_scr)

    ri = lax.broadcasted_iota(jnp.int32, (c, c), 0)
    ci = lax.broadcasted_iota(jnp.int32, (c, c), 1)
    masks = [ri == ci]
    for b in GLA_LEVELS:
        masks.append((ri // (2 * b) == ci // (2 * b)) & ((ri // b) % 2 == 1) & ((ci // b) % 2 == 0))
    amat = amat_ref[...]
    nt = (((1,), (1,)), ((), ()))

    def chunk(ic, carry):
        rows = pl.ds(pl.multiple_of(ic * c, c), c)
        for h in range(heads):
            qcols = slice(h * dk, (h + 1) * dk)
            kcols = slice(k_base + h * dk, k_base + (h + 1) * dk)
            vcols = slice(h * dv, (h + 1) * dv)
            q = qk_ref[rows, qcols].astype(F32) * (dk ** -0.5)
            k = qk_ref[rows, kcols].astype(F32)
            v = v_ref[rows, vcols]
            la = la_ref[rows, qcols]
            la_hi = la.astype(BF16)
            la_lo = (la - la_hi.astype(F32)).astype(BF16)
            expo = jnp.dot(amat, jnp.concatenate([la_hi, la_lo], axis=0), preferred_element_type=F32)
            e = jnp.exp(expo)
            state = s_scr[h]
            o = jnp.dot((q * e[0:c]).astype(BF16), state.astype(BF16), preferred_element_type=F32)

            scores = jnp.where(masks[0], lax.dot_general(q.astype(BF16), k.astype(BF16), nt,
                                                         preferred_element_type=F32), 0.0)
            for lvl in range(len(GLA_LEVELS)):
                eb = e[(2 + lvl) * c:(3 + lvl) * c]
                sb = lax.dot_general((q * eb).astype(BF16), (k * eb).astype(BF16), nt,
                                     preferred_element_type=F32)
                scores = scores + jnp.where(masks[1 + lvl], sb, 0.0)
            o = o + jnp.dot(scores.astype(BF16), v, preferred_element_type=F32)

            k_rem_t = (k * e[c:2 * c]).T.astype(BF16)
            e_last = jnp.exp(jnp.sum(la.T, axis=1, keepdims=True))
            s_scr[h] = state * e_last + jnp.dot(k_rem_t, v, preferred_element_type=F32)

            ms = jnp.mean(o * o, axis=-1, keepdims=True)
            r = r_ref[rows, vcols].astype(F32)
            o_ref[rows, vcols] = (o * lax.rsqrt(ms + EPS) * hg_ref[:, vcols]
                                  * (r * _sigmoid(r))).astype(BF16)
        return carry

    lax.fori_loop(0, lt // c, chunk, 0)


def _gla(proj, log_a, head_g, layer, batch, seq):
    t = proj.shape[0]
    dk_total = log_a.shape[1]
    dv_total = (proj.shape[1] - 2 * dk_total) // 2
    assert 2 * dk_total == dv_total
    lt = GLA_LT
    steps = seq // lt
    amat = jnp.asarray(_gla_decay_matrix(), BF16)
    row = lambda b, s: b * steps + s
    return pl.pallas_call(
        _gla_kernel,
        grid=(batch, steps),
        in_specs=[
            pl.BlockSpec((lt, 2 * dk_total), lambda b, s: (row(b, s), 0)),
            pl.BlockSpec((lt, dv_total), lambda b, s: (row(b, s), 1)),
            pl.BlockSpec((lt, dv_total), lambda b, s: (row(b, s), 2)),
            pl.BlockSpec((lt, dk_total), lambda b, s: (row(b, s), 0)),
            pl.BlockSpec(amat.shape, lambda b, s: (0, 0)),
            pl.BlockSpec((None, 1, dv_total), lambda b, s: (layer, 0, 0)),
        ],
        out_specs=pl.BlockSpec((lt, dv_total), lambda b, s: (row(b, s), 0)),
        out_shape=jax.ShapeDtypeStruct((t, dv_total), BF16),
        scratch_shapes=[pltpu.VMEM((GLA_HEADS, dk_total // GLA_HEADS, dv_total // GLA_HEADS), F32)],
        compiler_params=_params("parallel", "arbitrary"),
        name="gla_chunked",
    )(proj, proj, proj, log_a, amat, head_g)


def _out_proj_kernel(a_ref, w_ref, x_ref, o_ref):
    o_ref[...] = x_ref[...] + jnp.dot(a_ref[...], w_ref[...], preferred_element_type=F32)


def _out_proj(a, w, x, layer):
    t, d = x.shape
    kdim = a.shape[1]
    tm = TM_FFN
    return pl.pallas_call(
        _out_proj_kernel,
        grid=(t // tm,),
        in_specs=[
            pl.BlockSpec((tm, kdim), lambda i: (i, 0)),
            pl.BlockSpec((None, kdim, d), lambda i: (layer, 0, 0)),
            pl.BlockSpec((tm, d), lambda i: (i, 0)),
        ],
        out_specs=pl.BlockSpec((tm, d), lambda i: (i, 0)),
        out_shape=jax.ShapeDtypeStruct((t, d), F32),
        compiler_params=_params("parallel"),
        name="gla_out_proj",
    )(a, w, x)


def kernel(x, norm_mix_g, norm_ffn_g, conv_w_in, conv_b_in, conv_w_dw, conv_b_dw, conv_ln_g, conv_ln_b,
           conv_w_out, conv_b_out, gla_w_in, gla_gate_w1, gla_gate_w2, gla_gate_b, gla_head_norm_g,
           gla_w_out, ffn_w_gate_up, ffn_w_down, final_norm_g):
    batch, seq, d = x.shape
    depth = norm_mix_g.shape[0]
    assert seq % TM_PROJ == 0 and seq % TM_CONV == 0 and seq % GLA_LT == 0
    rows3d = lambda a: a.reshape(a.shape[0], 1, -1)
    conv_w_in, conv_w_out = conv_w_in.astype(BF16), conv_w_out.astype(BF16)
    gla_w_in, gla_w_out = gla_w_in.astype(BF16), gla_w_out.astype(BF16)
    gla_gate_w1, gla_gate_w2 = gla_gate_w1.astype(BF16), gla_gate_w2.astype(BF16)
    ffn_w_gate_up, ffn_w_down = ffn_w_gate_up.astype(BF16), ffn_w_down.astype(BF16)
    norm_ffn_g3 = rows3d(norm_ffn_g)
    final_g = final_norm_g.reshape(1, -1)
    h = x.reshape(batch * seq, d)
    for i in range(depth):
        j = i // 2
        g_mix = norm_mix_g[i].reshape(1, -1)
        if i % 2 == 0:
            u = _conv_in(h, g_mix, conv_w_in, rows3d(conv_b_in), j)
            h = _conv_mix(u, h, conv_w_dw, rows3d(conv_b_dw), rows3d(conv_ln_g), rows3d(conv_ln_b),
                          conv_w_out, rows3d(conv_b_out), j, seq)
        else:
            proj, log_a = _gla_proj(h, g_mix, gla_w_in, gla_gate_w1, gla_gate_w2, rows3d(gla_gate_b), j)
            a = _gla(proj, log_a, rows3d(gla_head_norm_g), j, batch, seq)
            h = _out_proj(a, gla_w_out, h, j)
        h = _ffn(h, norm_ffn_g3, ffn_w_gate_up, ffn_w_down, final_g, i, final_norm=(i == depth - 1))
    return h.reshape(batch, seq, d)
```

```python
import functools

import jax
import jax.numpy as jnp
import numpy as np
from jax import lax
from jax.experimental import pallas as pl
from jax.experimental.pallas import tpu as pltpu

F32 = jnp.float32
BF16 = jnp.bfloat16

EPS = 1e-6
CONV_WIDTH = 31
GLA_HEADS = 4
GLA_GATE_NORMALIZER = 16.0
LOG2_E = 1.4426950408889634

V7X_LANES = 128
V7X_SUBLANES = 8
V7X_VMEM_BYTES = 64 * 1024 * 1024

VMEM_LIMIT = V7X_VMEM_BYTES - 8 * 1024 * 1024
TM_PROJ = 1024
TM_FFN = 1024
TM_OUT = 512
TN_PROJ = 512
TN_GLA_PROJ = 1024
GATE_ROWS = 256
TM_CONV = 256
CONV_HALO = 32
CONV_ROWS = 128
GLA_CHUNK = 64
GLA_LT = 512
GLA_LEVELS = (32, 16, 8, 4, 2, 1)


def _params(*sem):
    return pltpu.CompilerParams(dimension_semantics=sem, vmem_limit_bytes=VMEM_LIMIT)


def _rms_norm_bf16(x, g):
    ms = jnp.mean(x * x, axis=-1, keepdims=True)
    return (x * lax.rsqrt(ms + EPS) * g).astype(BF16)


def _sigmoid(x):
    return 1.0 / (1.0 + jnp.exp(-x))


def _conv_in_kernel(x_ref, g_ref, wa_ref, wg_ref, ba_ref, bg_ref, u_ref, h_scr):
    @pl.when(pl.program_id(1) == 0)
    def _():
        h_scr[...] = _rms_norm_bf16(x_ref[...], g_ref[...])

    h = h_scr[...]
    a = jnp.dot(h, wa_ref[...], preferred_element_type=F32) + ba_ref[...]
    g = jnp.dot(h, wg_ref[...], preferred_element_type=F32) + bg_ref[...]
    u_ref[...] = a * _sigmoid(g)


def _conv_in(x, g, w_in, b_in, layer):
    t, d = x.shape
    tm, tn = TM_PROJ, TN_PROJ
    nj = d // tn
    return pl.pallas_call(
        _conv_in_kernel,
        grid=(t // tm, nj),
        in_specs=[
            pl.BlockSpec((tm, d), lambda i, j: (i, 0)),
            pl.BlockSpec((1, d), lambda i, j: (0, 0)),
            pl.BlockSpec((None, d, tn), lambda i, j: (layer, 0, j)),
            pl.BlockSpec((None, d, tn), lambda i, j: (layer, 0, j + nj)),
            pl.BlockSpec((None, 1, tn), lambda i, j: (layer, 0, j)),
            pl.BlockSpec((None, 1, tn), lambda i, j: (layer, 0, j + nj)),
        ],
        out_specs=pl.BlockSpec((tm, tn), lambda i, j: (i, j)),
        out_shape=jax.ShapeDtypeStruct((t, d), F32),
        scratch_shapes=[pltpu.VMEM((tm, d), BF16)],
        compiler_params=_params("parallel", "arbitrary"),
        name="conv_in_glu",
    )(x, g, w_in, w_in, b_in, b_in)


def _conv_mix_kernel(u_ref, halo_ref, x_ref, wdw_ref, bdw_ref, lng_ref, lnb_ref, wo_ref, bo_ref,
                     o_ref, ext_scr, sh_scr, c_scr, *, tiles_per_seq):
    tm, d = u_ref.shape
    sub = V7X_SUBLANES
    first = (pl.program_id(0) % tiles_per_seq) == 0
    ext_scr[0:CONV_HALO, :] = jnp.where(first, 0.0, halo_ref[...])
    ext_scr[CONV_HALO:, :] = u_ref[...]
    shift = CONV_HALO - (CONV_WIDTH - 1)
    groups = (tm + CONV_HALO) // sub
    sub_idx = lax.broadcasted_iota(jnp.int32, (groups - 1, sub, V7X_LANES), 1)

    def lane_block(c, carry):
        lanes = pl.ds(pl.multiple_of(c * V7X_LANES, V7X_LANES), V7X_LANES)
        x3 = ext_scr[:, lanes].reshape(groups, sub, V7X_LANES)
        for r in range(1, sub):
            rot = pltpu.roll(x3, sub - r, axis=1)
            y = jnp.where(sub_idx < sub - r, rot[:-1], rot[1:])
            sh_scr[r, 0:(groups - 1) * sub, :] = y.reshape((groups - 1) * sub, V7X_LANES)
        for r0 in range(0, tm, CONV_ROWS):
            acc = None
            for k in range(CONV_WIDTH):
                a, r = divmod(shift + k, sub)
                rows = pl.ds(r0 + a * sub, CONV_ROWS)
                seg = ext_scr[rows, lanes] if r == 0 else sh_scr[r, rows, :]
                term = seg * wdw_ref[pl.ds(k, 1), lanes]
                acc = term if acc is None else acc + term
            c_scr[pl.ds(r0, CONV_ROWS), lanes] = acc + bdw_ref[:, lanes]
        return carry

    lax.fori_loop(0, d // V7X_LANES, lane_block, 0)

    c = c_scr[...]
    mu = jnp.mean(c, axis=-1, keepdims=True)
    cc = c - mu
    var = jnp.mean(cc * cc, axis=-1, keepdims=True)
    y = cc * lax.rsqrt(var + EPS) * lng_ref[...] + lnb_ref[...]
    s = (y * _sigmoid(y)).astype(BF16)
    o_ref[...] = x_ref[...] + jnp.dot(s, wo_ref[...], preferred_element_type=F32) + bo_ref[...]


def _conv_mix(u, x, w_dw, b_dw, ln_g, ln_b, w_out, b_out, layer, seq):
    t, d = x.shape
    tm = TM_CONV
    halo_blocks = tm // CONV_HALO
    row = lambda i: (i, 0)
    fixed = lambda i: (layer, 0, 0)
    return pl.pallas_call(
        functools.partial(_conv_mix_kernel, tiles_per_seq=seq // tm),
        grid=(t // tm,),
        in_specs=[
            pl.BlockSpec((tm, d), row),
            pl.BlockSpec((CONV_HALO, d), lambda i: (jnp.maximum(i * halo_blocks - 1, 0), 0)),
            pl.BlockSpec((tm, d), row),
            pl.BlockSpec((None, CONV_WIDTH, d), fixed),
            pl.BlockSpec((None, 1, d), fixed),
            pl.BlockSpec((None, 1, d), fixed),
            pl.BlockSpec((None, 1, d), fixed),
            pl.BlockSpec((None, d, d), fixed),
            pl.BlockSpec((None, 1, d), fixed),
        ],
        out_specs=pl.BlockSpec((tm, d), row),
        out_shape=jax.ShapeDtypeStruct((t, d), F32),
        scratch_shapes=[
            pltpu.VMEM((tm + CONV_HALO, d), F32),
            pltpu.VMEM((V7X_SUBLANES, tm + CONV_HALO, V7X_LANES), F32),
            pltpu.VMEM((tm, d), F32),
        ],
        compiler_params=_params("parallel"),
        name="conv_dw_ln_out",
    )(u, u, x, w_dw, b_dw, ln_g, ln_b, w_out, b_out)


def _ffn_kernel(x_hbm, g_ref, wa_ref, wg_ref, wd_ref, fg_ref, o_ref, x_buf, h_scr, x_sem, *, final_norm):
    i, j = pl.program_id(0), pl.program_id(1)
    tm = x_buf.shape[0]

    def x_copy(tile):
        rows = pl.ds(pl.multiple_of(tile * tm, tm), tm)
        return pltpu.make_async_copy(x_hbm.at[rows], x_buf, x_sem)

    @pl.when(j == 0)
    def _():
        @pl.when(i == 0)
        def _():
            x_copy(i).start()

        x_copy(i).wait()
        x = x_buf[...]
        h_scr[...] = _rms_norm_bf16(x, g_ref[...])
        o_ref[...] = x

    @pl.when((j == 1) & (i + 1 < pl.num_programs(0)))
    def _():
        x_copy(i + 1).start()

    h = h_scr[...]
    a = jnp.dot(h, wa_ref[...], preferred_element_type=F32)
    g = jnp.dot(h, wg_ref[...], preferred_element_type=F32)
    p = (a * _sigmoid(a) * g).astype(BF16)
    o_ref[...] += jnp.dot(p, wd_ref[...], preferred_element_type=F32)

    if final_norm:
        @pl.when(j == pl.num_programs(1) - 1)
        def _():
            y = o_ref[...]
            ms = jnp.mean(y * y, axis=-1, keepdims=True)
            o_ref[...] = y * lax.rsqrt(ms + EPS) * fg_ref[...]


def _ffn(x, g, w_gate_up, w_down, final_g, layer, final_norm):
    t, d = x.shape
    d_ff = w_down.shape[1]
    tm, tf = TM_FFN, TN_PROJ
    nj = d_ff // tf
    assert nj >= 2
    return pl.pallas_call(
        functools.partial(_ffn_kernel, final_norm=final_norm),
        grid=(t // tm, nj),
        in_specs=[
            pl.BlockSpec(memory_space=pl.ANY),
            pl.BlockSpec((None, 1, d), lambda i, j: (layer, 0, 0)),
            pl.BlockSpec((None, d, tf), lambda i, j: (layer, 0, j)),
            pl.BlockSpec((None, d, tf), lambda i, j: (layer, 0, j + nj)),
            pl.BlockSpec((None, tf, d), lambda i, j: (layer, j, 0)),
            pl.BlockSpec((1, d), lambda i, j: (0, 0)),
        ],
        out_specs=pl.BlockSpec((tm, d), lambda i, j: (i, 0)),
        out_shape=jax.ShapeDtypeStruct((t, d), F32),
        scratch_shapes=[pltpu.VMEM((tm, d), F32), pltpu.VMEM((tm, d), BF16), pltpu.SemaphoreType.DMA(())],
        compiler_params=_params("arbitrary", "arbitrary"),
        name="ffn_swiglu",
    )(x, g, w_gate_up, w_gate_up, w_down, final_g)


def _gla_proj_kernel(x_ref, g_ref, w_ref, w1_ref, w2_ref, gb_ref, hg_ref, proj_ref, la_ref, h_scr, *,
                     gate_tile0):
    j = pl.program_id(1)

    @pl.when(j == 0)
    def _():
        for r0 in range(0, x_ref.shape[0], GATE_ROWS):
            rows = slice(r0, r0 + GATE_ROWS)
            h = _rms_norm_bf16(x_ref[rows, :], g_ref[...])
            h_scr[rows, :] = h
            low = jnp.dot(h, w1_ref[...], preferred_element_type=F32).astype(BF16)
            z = jnp.dot(low, w2_ref[...], preferred_element_type=F32) + gb_ref[...]
            log_sig = jnp.minimum(z, 0.0) - jnp.log(1.0 + jnp.exp(-jnp.abs(z)))
            la_ref[rows, :] = log_sig * (LOG2_E / GLA_GATE_NORMALIZER)

    acc = jnp.dot(h_scr[...], w_ref[...], preferred_element_type=F32)
    gated = acc * _sigmoid(acc) * hg_ref[...]
    proj_ref[...] = jnp.where(j >= gate_tile0, gated, acc).astype(BF16)


def _gla_proj(x, g, w_in, w1, w2, gate_b, head_g, layer):
    t, d = x.shape
    n = w_in.shape[2]
    dk = w2.shape[2]
    rank = w1.shape[2]
    dv = head_g.shape[2]
    tm, tn = TM_PROJ, TN_GLA_PROJ
    gate_tile0 = (n - dv) // tn
    fixed = lambda i, j: (layer, 0, 0)
    return pl.pallas_call(
        functools.partial(_gla_proj_kernel, gate_tile0=gate_tile0),
        grid=(t // tm, n // tn),
        in_specs=[
            pl.BlockSpec((tm, d), lambda i, j: (i, 0)),
            pl.BlockSpec((1, d), lambda i, j: (0, 0)),
            pl.BlockSpec((None, d, tn), lambda i, j: (layer, 0, j)),
            pl.BlockSpec((None, d, rank), fixed),
            pl.BlockSpec((None, rank, dk), fixed),
            pl.BlockSpec((None, 1, dk), fixed),
            pl.BlockSpec((None, 1, tn), lambda i, j: (layer, 0, jnp.maximum(j - gate_tile0, 0))),
        ],
        out_specs=[
            pl.BlockSpec((tm, tn), lambda i, j: (i, j)),
            pl.BlockSpec((tm, dk), lambda i, j: (i, 0)),
        ],
        out_shape=[jax.ShapeDtypeStruct((t, n), BF16), jax.ShapeDtypeStruct((t, dk), F32)],
        scratch_shapes=[pltpu.VMEM((tm, d), BF16)],
        compiler_params=_params("parallel", "arbitrary"),
        name="gla_in_proj",
    )(x, g, w_in, w1, w2, gate_b, head_g)


def _gla_decay_matrix():
    c = GLA_CHUNK
    i = np.arange(c)[:, None]
    t = np.arange(c)[None, :]
    blocks = [t <= i, t > i]
    for b in GLA_LEVELS:
        ref = (i // (2 * b)) * (2 * b) + b - 1
        blocks.append(np.where(i > ref, (t > ref) & (t <= i), (t > i) & (t <= ref)))
    a = np.concatenate(blocks, axis=0).astype(np.float32)
    return np.concatenate([a, a], axis=1)


def _gla_kernel(qk_ref, v_ref, rg_ref, la_ref, amat_ref, o_ref, s_scr):
    c = GLA_CHUNK
    lt = qk_ref.shape[0]
    heads = s_scr.shape[0]
    dk, dv = s_scr.shape[1], s_scr.shape[2]
    k_base = heads * dk

    @pl.when(pl.program_id(1) == 0)
    def _():
        s_scr[...] = jnp.zeros_like(s_scr)

    ri = lax.broadcasted_iota(jnp.int32, (c, c), 0)
    ci = lax.broadcasted_iota(jnp.int32, (c, c), 1)
    masks = [ri == ci]
    for b in GLA_LEVELS:
        masks.append((ri // (2 * b) == ci // (2 * b)) & ((ri // b) % 2 == 1) & ((ci // b) % 2 == 0))
    amat = amat_ref[...]
    nt = (((1,), (1,)), ((), ()))

    def chunk(ic, carry):
        rows = pl.ds(pl.multiple_of(ic * c, c), c)
        qcols = [slice(h * dk, (h + 1) * dk) for h in range(heads)]
        kcols = [slice(k_base + h * dk, k_base + (h + 1) * dk) for h in range(heads)]
        vcols = [slice(h * dv, (h + 1) * dv) for h in range(heads)]
        las = [la_ref[rows, qcols[h]] for h in range(heads)]
        e_rem, es = [], []
        for h in range(heads):
            la_hi = las[h].astype(BF16)
            la_lo = (las[h] - la_hi.astype(F32)).astype(BF16)
            expo = jnp.dot(amat, jnp.concatenate([la_hi, la_lo], axis=0), preferred_element_type=F32)
            e = jnp.exp2(expo)
            e_rem.append(e[c:2 * c])
            es.append(e.astype(BF16))
        qs = [qk_ref[rows, qcols[h]] for h in range(heads)]
        ks = [qk_ref[rows, kcols[h]] for h in range(heads)]
        scores = [jnp.where(masks[0], lax.dot_general(qs[h], ks[h], nt, preferred_element_type=F32), 0.0)
                  for h in range(heads)]
        for lvl in range(len(GLA_LEVELS)):
            for h in range(heads):
                eb = es[h][(2 + lvl) * c:(3 + lvl) * c]
                sb = lax.dot_general(qs[h] * eb, ks[h] * eb, nt, preferred_element_type=F32)
                scores[h] = scores[h] + jnp.where(masks[1 + lvl], sb, 0.0)
        for h in range(heads):
            v = v_ref[rows, vcols[h]]
            state = s_scr[h]
            o = jnp.dot(qs[h] * es[h][0:c], state.astype(BF16), preferred_element_type=F32)
            o = o + jnp.dot(scores[h].astype(BF16), v, preferred_element_type=F32)
            k_rem_t = (ks[h].astype(F32) * e_rem[h]).T.astype(BF16)
            e_last = jnp.exp2(jnp.sum(las[h].T, axis=1, keepdims=True))
            s_scr[h] = state * e_last + jnp.dot(k_rem_t, v, preferred_element_type=F32)
            ms = jnp.mean(o * o, axis=-1, keepdims=True)
            gate = rg_ref[rows, vcols[h]].astype(F32)
            o_ref[rows, vcols[h]] = (o * lax.rsqrt(ms + EPS * dk) * gate).astype(BF16)
        return carry

    lax.fori_loop(0, lt // c, chunk, 0)


def _gla(proj, log_a, batch, seq):
    t = proj.shape[0]
    dk_total = log_a.shape[1]
    dv_total = (proj.shape[1] - 2 * dk_total) // 2
    assert 2 * dk_total == dv_total
    lt = GLA_LT
    steps = seq // lt
    amat = jnp.asarray(_gla_decay_matrix(), BF16)
    row = lambda b, s: b * steps + s
    return pl.pallas_call(
        _gla_kernel,
        grid=(batch, steps),
        in_specs=[
            pl.BlockSpec((lt, 2 * dk_total), lambda b, s: (row(b, s), 0)),
            pl.BlockSpec((lt, dv_total), lambda b, s: (row(b, s), 1)),
            pl.BlockSpec((lt, dv_total), lambda b, s: (row(b, s), 2)),
            pl.BlockSpec((lt, dk_total), lambda b, s: (row(b, s), 0)),
            pl.BlockSpec(amat.shape, lambda b, s: (0, 0)),
        ],
        out_specs=pl.BlockSpec((lt, dv_total), lambda b, s: (row(b, s), 0)),
        out_shape=jax.ShapeDtypeStruct((t, dv_total), BF16),
        scratch_shapes=[pltpu.VMEM((GLA_HEADS, dk_total // GLA_HEADS, dv_total // GLA_HEADS), F32)],
        compiler_params=_params("parallel", "arbitrary"),
        name="gla_chunked",
    )(proj, proj, proj, log_a, amat)


def _out_proj_kernel(a_ref, w_ref, x_ref, o_ref):
    o_ref[...] = x_ref[...] + jnp.dot(a_ref[...], w_ref[...], preferred_element_type=F32)


def _out_proj(a, w, x, layer):
    t, d = x.shape
    kdim = a.shape[1]
    tm = TM_OUT
    return pl.pallas_call(
        _out_proj_kernel,
        grid=(t // tm,),
        in_specs=[
            pl.BlockSpec((tm, kdim), lambda i: (i, 0)),
            pl.BlockSpec((None, kdim, d), lambda i: (layer, 0, 0)),
            pl.BlockSpec((tm, d), lambda i: (i, 0)),
        ],
        out_specs=pl.BlockSpec((tm, d), lambda i: (i, 0)),
        out_shape=jax.ShapeDtypeStruct((t, d), F32),
        compiler_params=_params("parallel"),
        name="gla_out_proj",
    )(a, w, x)


def kernel(x, norm_mix_g, norm_ffn_g, conv_w_in, conv_b_in, conv_w_dw, conv_b_dw, conv_ln_g, conv_ln_b,
           conv_w_out, conv_b_out, gla_w_in, gla_gate_w1, gla_gate_w2, gla_gate_b, gla_head_norm_g,
           gla_w_out, ffn_w_gate_up, ffn_w_down, final_norm_g):
    batch, seq, d = x.shape
    depth = norm_mix_g.shape[0]
    assert all(seq % tile == 0 for tile in (TM_PROJ, TM_FFN, TM_OUT, TM_CONV, GLA_LT))
    rows3d = lambda a: a.reshape(a.shape[0], 1, -1)
    conv_w_in, conv_w_out = conv_w_in.astype(BF16), conv_w_out.astype(BF16)
    gla_w_in, gla_w_out = gla_w_in.astype(BF16), gla_w_out.astype(BF16)
    gla_gate_w1, gla_gate_w2 = gla_gate_w1.astype(BF16), gla_gate_w2.astype(BF16)
    ffn_w_gate_up, ffn_w_down = ffn_w_gate_up.astype(BF16), ffn_w_down.astype(BF16)
    norm_ffn_g3 = rows3d(norm_ffn_g)
    final_g = final_norm_g.reshape(1, -1)
    h = x.reshape(batch * seq, d)
    for i in range(depth):
        j = i // 2
        g_mix = norm_mix_g[i].reshape(1, -1)
        if i % 2 == 0:
            u = _conv_in(h, g_mix, conv_w_in, rows3d(conv_b_in), j)
            h = _conv_mix(u, h, conv_w_dw, rows3d(conv_b_dw), rows3d(conv_ln_g), rows3d(conv_ln_b),
                          conv_w_out, rows3d(conv_b_out), j, seq)
        else:
            proj, log_a = _gla_proj(h, g_mix, gla_w_in, gla_gate_w1, gla_gate_w2, rows3d(gla_gate_b),
                                    rows3d(gla_head_norm_g), j)
            a = _gla(proj, log_a, batch, seq)
            h = _out_proj(a, gla_w_out, h, j)
        h = _ffn(h, norm_ffn_g3, ffn_w_gate_up, ffn_w_down, final_g, i, final_norm=(i == depth - 1))
    return h.reshape(batch, seq, d)
```

```python
import functools

import jax
import jax.numpy as jnp
import numpy as np
from jax import lax
from jax.experimental import pallas as pl
from jax.experimental.pallas import tpu as pltpu

F32 = jnp.float32
BF16 = jnp.bfloat16

EPS = 1e-6
CONV_WIDTH = 31
GLA_HEADS = 4
GLA_GATE_NORMALIZER = 16.0
LOG2_E = 1.4426950408889634

V7X_LANES = 128
V7X_SUBLANES = 8
V7X_VMEM_BYTES = 64 * 1024 * 1024

VMEM_LIMIT = V7X_VMEM_BYTES - 8 * 1024 * 1024
TM_PROJ = 1024
TM_FFN = 1024
TM_OUT = 512
TN_PROJ = 512
TN_GLA_PROJ = 1024
GATE_ROWS = 256
TM_CONV = 256
CONV_CB = 256
CONV_HALO = 32
CONV_ROWS = 128
GLA_CHUNK = 64
GLA_LT = 512
GLA_LEVELS = (32, 16, 8, 4, 2, 1)


def _params(*sem):
    return pltpu.CompilerParams(dimension_semantics=sem, vmem_limit_bytes=VMEM_LIMIT)


def _rms_norm_bf16(x, g):
    ms = jnp.mean(x * x, axis=-1, keepdims=True)
    return (x * lax.rsqrt(ms + EPS) * g).astype(BF16)


def _sigmoid(x):
    return 1.0 / (1.0 + jnp.exp(-x))


def _conv_module_kernel(x_ref, xp_ref, g_ref, win_ref, bin_ref, wdw_ref, bdw_ref, lng_ref, lnb_ref,
                        wo_ref, bo_ref, o_ref, h_scr, ext_scr, sh_scr, c_scr, s_scr, *, tiles_per_seq):
    i = pl.program_id(0)
    tm, d = x_ref.shape
    sub, lanes_w = V7X_SUBLANES, V7X_LANES
    nblk = d // CONV_CB
    shift = CONV_HALO - (CONV_WIDTH - 1)
    groups = (tm + CONV_HALO) // sub
    sub_idx = lax.broadcasted_iota(jnp.int32, (groups - 1, sub, lanes_w), 1)

    @pl.when(i == 0)
    def _():
        s_scr[...] = jnp.zeros_like(s_scr)
        ext_scr[tm:, :] = jnp.zeros((CONV_HALO, d), F32)

    first = (i % tiles_per_seq) == 0
    ext_scr[0:CONV_HALO, :] = jnp.where(first, 0.0, ext_scr[tm:, :])
    h_scr[...] = _rms_norm_bf16(x_ref[...], g_ref[...])

    def in_proj(cb):
        cols = pl.ds(cb * CONV_CB, CONV_CB)
        gcols = pl.ds(d + cb * CONV_CB, CONV_CB)
        h = h_scr[...]
        a = jnp.dot(h, win_ref[:, cols], preferred_element_type=F32) + bin_ref[:, cols]
        g = jnp.dot(h, win_ref[:, gcols], preferred_element_type=F32) + bin_ref[:, gcols]
        ext_scr[CONV_HALO:, cols] = a * _sigmoid(g)

    def conv(cb):
        for part in range(CONV_CB // lanes_w):
            lanes = pl.ds(cb * CONV_CB + part * lanes_w, lanes_w)
            x3 = ext_scr[:, lanes].reshape(groups, sub, lanes_w)
            for r in range(1, sub):
                rot = pltpu.roll(x3, sub - r, axis=1)
                y = jnp.where(sub_idx < sub - r, rot[:-1], rot[1:])
                sh_scr[part, r, 0:(groups - 1) * sub, :] = y.reshape((groups - 1) * sub, lanes_w)
            for r0 in range(0, tm, CONV_ROWS):
                acc = None
                for k in range(CONV_WIDTH):
                    a, r = divmod(shift + k, sub)
                    rows = pl.ds(r0 + a * sub, CONV_ROWS)
                    seg = ext_scr[rows, lanes] if r == 0 else sh_scr[part, r, rows, :]
                    term = seg * wdw_ref[pl.ds(k, 1), lanes]
                    acc = term if acc is None else acc + term
                c_scr[pl.ds(r0, CONV_ROWS), lanes] = acc + bdw_ref[:, lanes]

    def out_proj(cb):
        cols = pl.ds(cb * CONV_CB, CONV_CB)
        o_ref[:, cols] = (xp_ref[:, cols] + bo_ref[:, cols]
                          + jnp.dot(s_scr[...], wo_ref[:, cols], preferred_element_type=F32))

    in_proj(0)
    for cb in range(nblk):
        if cb + 1 < nblk:
            in_proj(cb + 1)
        conv(cb)
        out_proj(cb)

    c = c_scr[...]
    mu = jnp.mean(c, axis=-1, keepdims=True)
    cc = c - mu
    var = jnp.mean(cc * cc, axis=-1, keepdims=True)
    y = cc * lax.rsqrt(var + EPS) * lng_ref[...] + lnb_ref[...]
    s_scr[...] = (y * _sigmoid(y)).astype(BF16)


def _conv_module(x, g, w_in, b_in, w_dw, b_dw, ln_g, ln_b, w_out, b_out, layer, seq):
    t, d = x.shape
    tm = TM_CONV
    n = t // tm
    cur = lambda i: (jnp.minimum(i, n - 1), 0)
    prev = lambda i: (jnp.maximum(i - 1, 0), 0)
    fixed = lambda i: (layer, 0, 0)
    once = pl.Buffered(1)
    return pl.pallas_call(
        functools.partial(_conv_module_kernel, tiles_per_seq=seq // tm),
        grid=(n + 1,),
        in_specs=[
            pl.BlockSpec((tm, d), cur),
            pl.BlockSpec((tm, d), prev),
            pl.BlockSpec((1, d), lambda i: (0, 0)),
            pl.BlockSpec((None, d, 2 * d), fixed, pipeline_mode=once),
            pl.BlockSpec((None, 1, 2 * d), fixed),
            pl.BlockSpec((None, CONV_WIDTH, d), fixed),
            pl.BlockSpec((None, 1, d), fixed),
            pl.BlockSpec((None, 1, d), fixed),
            pl.BlockSpec((None, 1, d), fixed),
            pl.BlockSpec((None, d, d), fixed, pipeline_mode=once),
            pl.BlockSpec((None, 1, d), fixed),
        ],
        out_specs=pl.BlockSpec((tm, d), prev),
        out_shape=jax.ShapeDtypeStruct((t, d), F32),
        scratch_shapes=[
            pltpu.VMEM((tm, d), BF16),
            pltpu.VMEM((tm + CONV_HALO, d), F32),
            pltpu.VMEM((CONV_CB // V7X_LANES, V7X_SUBLANES, tm + CONV_HALO, V7X_LANES), F32),
            pltpu.VMEM((tm, d), F32),
            pltpu.VMEM((tm, d), BF16),
        ],
        compiler_params=_params("arbitrary"),
        name="conv_module",
    )(x, x, g, w_in, b_in, w_dw, b_dw, ln_g, ln_b, w_out, b_out)


def _ffn_kernel(x_hbm, g_ref, wa_ref, wg_ref, wd_ref, fg_ref, o_ref, x_buf, h_scr, x_sem, *, final_norm):
    i, j = pl.program_id(0), pl.program_id(1)
    tm = x_buf.shape[0]

    def x_copy(tile):
        rows = pl.ds(pl.multiple_of(tile * tm, tm), tm)
        return pltpu.make_async_copy(x_hbm.at[rows], x_buf, x_sem)

    @pl.when(j == 0)
    def _():
        @pl.when(i == 0)
        def _():
            x_copy(i).start()

        x_copy(i).wait()
        x = x_buf[...]
        h_scr[...] = _rms_norm_bf16(x, g_ref[...])
        o_ref[...] = x

    @pl.when((j == 1) & (i + 1 < pl.num_programs(0)))
    def _():
        x_copy(i + 1).start()

    h = h_scr[...]
    a = jnp.dot(h, wa_ref[...], preferred_element_type=F32)
    g = jnp.dot(h, wg_ref[...], preferred_element_type=F32)
    p = (a * _sigmoid(a) * g).astype(BF16)
    o_ref[...] += jnp.dot(p, wd_ref[...], preferred_element_type=F32)

    if final_norm:
        @pl.when(j == pl.num_programs(1) - 1)
        def _():
            y = o_ref[...]
            ms = jnp.mean(y * y, axis=-1, keepdims=True)
            o_ref[...] = y * lax.rsqrt(ms + EPS) * fg_ref[...]


def _ffn(x, g, w_gate_up, w_down, final_g, layer, final_norm):
    t, d = x.shape
    d_ff = w_down.shape[1]
    tm, tf = TM_FFN, TN_PROJ
    nj = d_ff // tf
    assert nj >= 2
    return pl.pallas_call(
        functools.partial(_ffn_kernel, final_norm=final_norm),
        grid=(t // tm, nj),
        in_specs=[
            pl.BlockSpec(memory_space=pl.ANY),
            pl.BlockSpec((None, 1, d), lambda i, j: (layer, 0, 0)),
            pl.BlockSpec((None, d, tf), lambda i, j: (layer, 0, j)),
            pl.BlockSpec((None, d, tf), lambda i, j: (layer, 0, j + nj)),
            pl.BlockSpec((None, tf, d), lambda i, j: (layer, j, 0)),
            pl.BlockSpec((1, d), lambda i, j: (0, 0)),
        ],
        out_specs=pl.BlockSpec((tm, d), lambda i, j: (i, 0)),
        out_shape=jax.ShapeDtypeStruct((t, d), F32),
        scratch_shapes=[pltpu.VMEM((tm, d), F32), pltpu.VMEM((tm, d), BF16), pltpu.SemaphoreType.DMA(())],
        compiler_params=_params("arbitrary", "arbitrary"),
        name="ffn_swiglu",
    )(x, g, w_gate_up, w_gate_up, w_down, final_g)


def _gla_proj_kernel(x_ref, g_ref, w_ref, w1_ref, w2_ref, gb_ref, hg_ref, proj_ref, la_ref, h_scr, *,
                     gate_tile0):
    j = pl.program_id(1)

    @pl.when(j == 0)
    def _():
        for r0 in range(0, x_ref.shape[0], GATE_ROWS):
            rows = slice(r0, r0 + GATE_ROWS)
            h = _rms_norm_bf16(x_ref[rows, :], g_ref[...])
            h_scr[rows, :] = h
            low = jnp.dot(h, w1_ref[...], preferred_element_type=F32).astype(BF16)
            z = jnp.dot(low, w2_ref[...], preferred_element_type=F32) + gb_ref[...]
            log_sig = jnp.minimum(z, 0.0) - jnp.log(1.0 + jnp.exp(-jnp.abs(z)))
            la_ref[rows, :] = log_sig * (LOG2_E / GLA_GATE_NORMALIZER)

    acc = jnp.dot(h_scr[...], w_ref[...], preferred_element_type=F32)
    gated = acc * _sigmoid(acc) * hg_ref[...]
    proj_ref[...] = jnp.where(j >= gate_tile0, gated, acc).astype(BF16)


def _gla_proj(x, g, w_in, w1, w2, gate_b, head_g, layer):
    t, d = x.shape
    n = w_in.shape[2]
    dk = w2.shape[2]
    rank = w1.shape[2]
    dv = head_g.shape[2]
    tm, tn = TM_PROJ, TN_GLA_PROJ
    gate_tile0 = (n - dv) // tn
    fixed = lambda i, j: (layer, 0, 0)
    return pl.pallas_call(
        functools.partial(_gla_proj_kernel, gate_tile0=gate_tile0),
        grid=(t // tm, n // tn),
        in_specs=[
            pl.BlockSpec((tm, d), lambda i, j: (i, 0)),
            pl.BlockSpec((1, d), lambda i, j: (0, 0)),
            pl.BlockSpec((None, d, tn), lambda i, j: (layer, 0, j)),
            pl.BlockSpec((None, d, rank), fixed),
            pl.BlockSpec((None, rank, dk), fixed),
            pl.BlockSpec((None, 1, dk), fixed),
            pl.BlockSpec((None, 1, tn), lambda i, j: (layer, 0, jnp.maximum(j - gate_tile0, 0))),
        ],
        out_specs=[
            pl.BlockSpec((tm, tn), lambda i, j: (i, j)),
            pl.BlockSpec((tm, dk), lambda i, j: (i, 0)),
        ],
        out_shape=[jax.ShapeDtypeStruct((t, n), BF16), jax.ShapeDtypeStruct((t, dk), F32)],
        scratch_shapes=[pltpu.VMEM((tm, d), BF16)],
        compiler_params=_params("parallel", "arbitrary"),
        name="gla_in_proj",
    )(x, g, w_in, w1, w2, gate_b, head_g)


def _gla_decay_matrix():
    c = GLA_CHUNK
    i = np.arange(c)[:, None]
    t = np.arange(c)[None, :]
    blocks = [t <= i, t > i]
    for b in GLA_LEVELS:
        ref = (i // (2 * b)) * (2 * b) + b - 1
        blocks.append(np.where(i > ref, (t > ref) & (t <= i), (t > i) & (t <= ref)))
    a = np.concatenate(blocks, axis=0).astype(np.float32)
    return np.concatenate([a, a], axis=1)


def _gla_kernel(qk_ref, v_ref, rg_ref, la_ref, amat_ref, o_ref, s_scr):
    c = GLA_CHUNK
    lt = qk_ref.shape[0]
    heads = s_scr.shape[0]
    dk, dv = s_scr.shape[1], s_scr.shape[2]
    k_base = heads * dk

    @pl.when(pl.program_id(1) == 0)
    def _():
        s_scr[...] = jnp.zeros_like(s_scr)

    ri = lax.broadcasted_iota(jnp.int32, (c, c), 0)
    ci = lax.broadcasted_iota(jnp.int32, (c, c), 1)
    masks = [ri == ci]
    for b in GLA_LEVELS:
        masks.append((ri // (2 * b) == ci // (2 * b)) & ((ri // b) % 2 == 1) & ((ci // b) % 2 == 0))
    amat = amat_ref[...]
    nt = (((1,), (1,)), ((), ()))

    def chunk(ic, carry):
        rows = pl.ds(pl.multiple_of(ic * c, c), c)
        qcols = [slice(h * dk, (h + 1) * dk) for h in range(heads)]
        kcols = [slice(k_base + h * dk, k_base + (h + 1) * dk) for h in range(heads)]
        vcols = [slice(h * dv, (h + 1) * dv) for h in range(heads)]
        las = [la_ref[rows, qcols[h]] for h in range(heads)]
        e_rem, es = [], []
        for h in range(heads):
            la_hi = las[h].astype(BF16)
            la_lo = (las[h] - la_hi.astype(F32)).astype(BF16)
            expo = jnp.dot(amat, jnp.concatenate([la_hi, la_lo], axis=0), preferred_element_type=F32)
            e = jnp.exp2(expo)
            e_rem.append(e[c:2 * c])
            es.append(e.astype(BF16))
        qs = [qk_ref[rows, qcols[h]] for h in range(heads)]
        ks = [qk_ref[rows, kcols[h]] for h in range(heads)]
        scores = [jnp.where(masks[0], lax.dot_general(qs[h], ks[h], nt, preferred_element_type=F32), 0.0)
                  for h in range(heads)]
        for lvl in range(len(GLA_LEVELS)):
            for h in range(heads):
                eb = es[h][(2 + lvl) * c:(3 + lvl) * c]
                sb = lax.dot_general(qs[h] * eb, ks[h] * eb, nt, preferred_element_type=F32)
                scores[h] = scores[h] + jnp.where(masks[1 + lvl], sb, 0.0)
        for h in range(heads):
            v = v_ref[rows, vcols[h]]
            state = s_scr[h]
            o = jnp.dot(qs[h] * es[h][0:c], state.astype(BF16), preferred_element_type=F32)
            o = o + jnp.dot(scores[h].astype(BF16), v, preferred_element_type=F32)
            k_rem_t = (ks[h].astype(F32) * e_rem[h]).T.astype(BF16)
            e_last = jnp.exp2(jnp.sum(las[h].T, axis=1, keepdims=True))
            s_scr[h] = state * e_last + jnp.dot(k_rem_t, v, preferred_element_type=F32)
            ms = jnp.mean(o * o, axis=-1, keepdims=True)
            gate = rg_ref[rows, vcols[h]].astype(F32)
            o_ref[rows, vcols[h]] = (o * lax.rsqrt(ms + EPS * dk) * gate).astype(BF16)
        return carry

    lax.fori_loop(0, lt // c, chunk, 0)


def _gla(proj, log_a, batch, seq):
    t = proj.shape[0]
    dk_total = log_a.shape[1]
    dv_total = (proj.shape[1] - 2 * dk_total) // 2
    assert 2 * dk_total == dv_total
    lt = GLA_LT
    steps = seq // lt
    amat = jnp.asarray(_gla_decay_matrix(), BF16)
    row = lambda b, s: b * steps + s
    return pl.pallas_call(
        _gla_kernel,
        grid=(batch, steps),
        in_specs=[
            pl.BlockSpec((lt, 2 * dk_total), lambda b, s: (row(b, s), 0)),
            pl.BlockSpec((lt, dv_total), lambda b, s: (row(b, s), 1)),
            pl.BlockSpec((lt, dv_total), lambda b, s: (row(b, s), 2)),
            pl.BlockSpec((lt, dk_total), lambda b, s: (row(b, s), 0)),
            pl.BlockSpec(amat.shape, lambda b, s: (0, 0)),
        ],
        out_specs=pl.BlockSpec((lt, dv_total), lambda b, s: (row(b, s), 0)),
        out_shape=jax.ShapeDtypeStruct((t, dv_total), BF16),
        scratch_shapes=[pltpu.VMEM((GLA_HEADS, dk_total // GLA_HEADS, dv_total // GLA_HEADS), F32)],
        compiler_params=_params("parallel", "arbitrary"),
        name="gla_chunked",
    )(proj, proj, proj, log_a, amat)


def _out_proj_kernel(a_ref, w_ref, x_ref, o_ref):
    o_ref[...] = x_ref[...] + jnp.dot(a_ref[...], w_ref[...], preferred_element_type=F32)


def _out_proj(a, w, x, layer):
    t, d = x.shape
    kdim = a.shape[1]
    tm = TM_OUT
    return pl.pallas_call(
        _out_proj_kernel,
        grid=(t // tm,),
        in_specs=[
            pl.BlockSpec((tm, kdim), lambda i: (i, 0)),
            pl.BlockSpec((None, kdim, d), lambda i: (layer, 0, 0)),
            pl.BlockSpec((tm, d), lambda i: (i, 0)),
        ],
        out_specs=pl.BlockSpec((tm, d), lambda i: (i, 0)),
        out_shape=jax.ShapeDtypeStruct((t, d), F32),
        compiler_params=_params("parallel"),
        name="gla_out_proj",
    )(a, w, x)


def kernel(x, norm_mix_g, norm_ffn_g, conv_w_in, conv_b_in, conv_w_dw, conv_b_dw, conv_ln_g, conv_ln_b,
           conv_w_out, conv_b_out, gla_w_in, gla_gate_w1, gla_gate_w2, gla_gate_b, gla_head_norm_g,
           gla_w_out, ffn_w_gate_up, ffn_w_down, final_norm_g):
    batch, seq, d = x.shape
    depth = norm_mix_g.shape[0]
    assert all(seq % tile == 0 for tile in (TM_PROJ, TM_FFN, TM_OUT, TM_CONV, GLA_LT))
    rows3d = lambda a: a.reshape(a.shape[0], 1, -1)
    conv_w_in, conv_w_out = conv_w_in.astype(BF16), conv_w_out.astype(BF16)
    gla_w_in, gla_w_out = gla_w_in.astype(BF16), gla_w_out.astype(BF16)
    gla_gate_w1, gla_gate_w2 = gla_gate_w1.astype(BF16), gla_gate_w2.astype(BF16)
    ffn_w_gate_up, ffn_w_down = ffn_w_gate_up.astype(BF16), ffn_w_down.astype(BF16)
    norm_ffn_g3 = rows3d(norm_ffn_g)
    final_g = final_norm_g.reshape(1, -1)
    h = x.reshape(batch * seq, d)
    for i in range(depth):
        j = i // 2
        g_mix = norm_mix_g[i].reshape(1, -1)
        if i % 2 == 0:
            h = _conv_module(h, g_mix, conv_w_in, rows3d(conv_b_in), conv_w_dw, rows3d(conv_b_dw),
                             rows3d(conv_ln_g), rows3d(conv_ln_b), conv_w_out, rows3d(conv_b_out), j, seq)
        else:
            proj, log_a = _gla_proj(h, g_mix, gla_w_in, gla_gate_w1, gla_gate_w2, rows3d(gla_gate_b),
                                    rows3d(gla_head_norm_g), j)
            a = _gla(proj, log_a, batch, seq)
            h = _out_proj(a, gla_w_out, h, j)
        h = _ffn(h, norm_ffn_g3, ffn_w_gate_up, ffn_w_down, final_g, i, final_norm=(i == depth - 1))
    return h.reshape(batch, seq, d)
```

```python
import functools

import jax
import jax.numpy as jnp
import numpy as np
from jax import lax
from jax.experimental import pallas as pl
from jax.experimental.pallas import tpu as pltpu

F32 = jnp.float32
BF16 = jnp.bfloat16

EPS = 1e-6
CONV_WIDTH = 31
GLA_HEADS = 4
GLA_GATE_NORMALIZER = 16.0
LOG2_E = 1.4426950408889634

V7X_LANES = 128
V7X_SUBLANES = 8
V7X_MXU_COLS = 256
V7X_VMEM_BYTES = 64 * 1024 * 1024

VMEM_LIMIT = V7X_VMEM_BYTES - 8 * 1024 * 1024
TM_PROJ = 1024
TM_FFN = 1024
FFN_ROWS = 256
TN_PROJ = 512
TN_GLA_PROJ = 1024
GATE_ROWS = 256
TM_CONV = 256
CONV_CB = 256
CONV_HALO = 32
CONV_ROWS = 128
GLA_CHUNK = 64
GLA_LT = 512
GLA_LEVELS = (32, 16, 8, 4, 2, 1)


def _params(*sem):
    return pltpu.CompilerParams(dimension_semantics=sem, vmem_limit_bytes=VMEM_LIMIT)


def _rms_norm_bf16(x, g):
    ms = jnp.mean(x * x, axis=-1, keepdims=True)
    return (x * lax.rsqrt(ms + EPS) * g).astype(BF16)


def _sigmoid(x):
    return 1.0 / (1.0 + jnp.exp(-x))


def _conv_module_kernel(x_ref, xp_ref, g_ref, win_ref, bin_ref, wdw_ref, bdw_ref, lng_ref, lnb_ref,
                        wo_ref, bo_ref, o_ref, h_scr, ext_scr, sh_scr, c_scr, s_scr, *, tiles_per_seq):
    i = pl.program_id(0)
    tm, d = x_ref.shape
    sub, lanes_w = V7X_SUBLANES, V7X_LANES
    nblk = d // CONV_CB
    shift = CONV_HALO - (CONV_WIDTH - 1)
    groups = (tm + CONV_HALO) // sub
    sub_idx = lax.broadcasted_iota(jnp.int32, (groups - 1, sub, lanes_w), 1)

    @pl.when(i == 0)
    def _():
        s_scr[...] = jnp.zeros_like(s_scr)
        ext_scr[tm:, :] = jnp.zeros((CONV_HALO, d), F32)

    first = (i % tiles_per_seq) == 0
    ext_scr[0:CONV_HALO, :] = jnp.where(first, 0.0, ext_scr[tm:, :])
    h_scr[...] = _rms_norm_bf16(x_ref[...], g_ref[...])

    def in_proj(cb):
        cols = pl.ds(cb * CONV_CB, CONV_CB)
        gcols = pl.ds(d + cb * CONV_CB, CONV_CB)
        h = h_scr[...]
        a = jnp.dot(h, win_ref[:, cols], preferred_element_type=F32) + bin_ref[:, cols]
        g = jnp.dot(h, win_ref[:, gcols], preferred_element_type=F32) + bin_ref[:, gcols]
        ext_scr[CONV_HALO:, cols] = a * _sigmoid(g)

    def conv(cb):
        for part in range(CONV_CB // lanes_w):
            lanes = pl.ds(cb * CONV_CB + part * lanes_w, lanes_w)
            x3 = ext_scr[:, lanes].reshape(groups, sub, lanes_w)
            for r in range(1, sub):
                rot = pltpu.roll(x3, sub - r, axis=1)
                y = jnp.where(sub_idx < sub - r, rot[:-1], rot[1:])
                sh_scr[part, r, 0:(groups - 1) * sub, :] = y.reshape((groups - 1) * sub, lanes_w)
            for r0 in range(0, tm, CONV_ROWS):
                acc = None
                for k in range(CONV_WIDTH):
                    a, r = divmod(shift + k, sub)
                    rows = pl.ds(r0 + a * sub, CONV_ROWS)
                    seg = ext_scr[rows, lanes] if r == 0 else sh_scr[part, r, rows, :]
                    term = seg * wdw_ref[pl.ds(k, 1), lanes]
                    acc = term if acc is None else acc + term
                c_scr[pl.ds(r0, CONV_ROWS), lanes] = acc + bdw_ref[:, lanes]

    def out_proj(cb):
        cols = pl.ds(cb * CONV_CB, CONV_CB)
        o_ref[:, cols] = (xp_ref[:, cols] + bo_ref[:, cols]
                          + jnp.dot(s_scr[...], wo_ref[:, cols], preferred_element_type=F32))

    in_proj(0)
    for cb in range(nblk):
        if cb + 1 < nblk:
            in_proj(cb + 1)
        conv(cb)
        out_proj(cb)

    c = c_scr[...]
    mu = jnp.mean(c, axis=-1, keepdims=True)
    cc = c - mu
    var = jnp.mean(cc * cc, axis=-1, keepdims=True)
    y = cc * lax.rsqrt(var + EPS) * lng_ref[...] + lnb_ref[...]
    s_scr[...] = (y * _sigmoid(y)).astype(BF16)


def _conv_module(x, g, w_in, b_in, w_dw, b_dw, ln_g, ln_b, w_out, b_out, layer, seq):
    t, d = x.shape
    tm = TM_CONV
    n = t // tm
    cur = lambda i: (jnp.minimum(i, n - 1), 0)
    prev = lambda i: (jnp.maximum(i - 1, 0), 0)
    fixed = lambda i: (layer, 0, 0)
    once = pl.Buffered(1)
    return pl.pallas_call(
        functools.partial(_conv_module_kernel, tiles_per_seq=seq // tm),
        grid=(n + 1,),
        in_specs=[
            pl.BlockSpec((tm, d), cur),
            pl.BlockSpec((tm, d), prev),
            pl.BlockSpec((1, d), lambda i: (0, 0)),
            pl.BlockSpec((None, d, 2 * d), fixed, pipeline_mode=once),
            pl.BlockSpec((None, 1, 2 * d), fixed),
            pl.BlockSpec((None, CONV_WIDTH, d), fixed),
            pl.BlockSpec((None, 1, d), fixed),
            pl.BlockSpec((None, 1, d), fixed),
            pl.BlockSpec((None, 1, d), fixed),
            pl.BlockSpec((None, d, d), fixed, pipeline_mode=once),
            pl.BlockSpec((None, 1, d), fixed),
        ],
        out_specs=pl.BlockSpec((tm, d), prev),
        out_shape=jax.ShapeDtypeStruct((t, d), F32),
        scratch_shapes=[
            pltpu.VMEM((tm, d), BF16),
            pltpu.VMEM((tm + CONV_HALO, d), F32),
            pltpu.VMEM((CONV_CB // V7X_LANES, V7X_SUBLANES, tm + CONV_HALO, V7X_LANES), F32),
            pltpu.VMEM((tm, d), F32),
            pltpu.VMEM((tm, d), BF16),
        ],
        compiler_params=_params("arbitrary"),
        name="conv_module",
    )(x, x, g, w_in, b_in, w_dw, b_dw, ln_g, ln_b, w_out, b_out)


def _ffn_kernel(x_hbm, g_ref, wa_ref, wg_ref, wd_ref, fg_ref, o_ref, x_buf, h_scr, x_sem, *, final_norm):
    i, j = pl.program_id(0), pl.program_id(1)
    tm = x_buf.shape[0]

    def x_copy(tile):
        rows = pl.ds(pl.multiple_of(tile * tm, tm), tm)
        return pltpu.make_async_copy(x_hbm.at[rows], x_buf, x_sem)

    def partial_out(h):
        a = jnp.dot(h, wa_ref[...], preferred_element_type=F32)
        g = jnp.dot(h, wg_ref[...], preferred_element_type=F32)
        p = (a * _sigmoid(a) * g).astype(BF16)
        return jnp.dot(p, wd_ref[...], preferred_element_type=F32)

    @pl.when(j == 0)
    def _():
        @pl.when(i == 0)
        def _():
            x_copy(i).start()

        x_copy(i).wait()
        for r0 in range(0, tm, FFN_ROWS):
            rows = slice(r0, r0 + FFN_ROWS)
            x = x_buf[rows, :]
            h = _rms_norm_bf16(x, g_ref[...])
            h_scr[rows, :] = h
            o_ref[rows, :] = x + partial_out(h)

    @pl.when((j == 1) & (i + 1 < pl.num_programs(0)))
    def _():
        x_copy(i + 1).start()

    @pl.when(j > 0)
    def _():
        o_ref[...] += partial_out(h_scr[...])

    if final_norm:
        @pl.when(j == pl.num_programs(1) - 1)
        def _():
            y = o_ref[...]
            ms = jnp.mean(y * y, axis=-1, keepdims=True)
            o_ref[...] = y * lax.rsqrt(ms + EPS) * fg_ref[...]


def _ffn(x, g, w_gate_up, w_down, final_g, layer, final_norm):
    t, d = x.shape
    d_ff = w_down.shape[1]
    tm, tf = TM_FFN, TN_PROJ
    nj = d_ff // tf
    assert nj >= 2
    return pl.pallas_call(
        functools.partial(_ffn_kernel, final_norm=final_norm),
        grid=(t // tm, nj),
        in_specs=[
            pl.BlockSpec(memory_space=pl.ANY),
            pl.BlockSpec((None, 1, d), lambda i, j: (layer, 0, 0)),
            pl.BlockSpec((None, d, tf), lambda i, j: (layer, 0, j)),
            pl.BlockSpec((None, d, tf), lambda i, j: (layer, 0, j + nj)),
            pl.BlockSpec((None, tf, d), lambda i, j: (layer, j, 0)),
            pl.BlockSpec((1, d), lambda i, j: (0, 0)),
        ],
        out_specs=pl.BlockSpec((tm, d), lambda i, j: (i, 0)),
        out_shape=jax.ShapeDtypeStruct((t, d), F32),
        scratch_shapes=[pltpu.VMEM((tm, d), F32), pltpu.VMEM((tm, d), BF16), pltpu.SemaphoreType.DMA(())],
        compiler_params=_params("arbitrary", "arbitrary"),
        name="ffn_swiglu",
    )(x, g, w_gate_up, w_gate_up, w_down, final_g)


def _gla_proj_kernel(x_ref, g_ref, w_ref, w1_ref, w2_ref, gb_ref, hg_ref, proj_ref, la_ref, h_scr, *,
                     gate_tile0):
    j = pl.program_id(1)

    @pl.when(j == 0)
    def _():
        for r0 in range(0, x_ref.shape[0], GATE_ROWS):
            rows = slice(r0, r0 + GATE_ROWS)
            h = _rms_norm_bf16(x_ref[rows, :], g_ref[...])
            h_scr[rows, :] = h
            low = jnp.dot(h, w1_ref[...], preferred_element_type=F32).astype(BF16)
            z = jnp.dot(low, w2_ref[...], preferred_element_type=F32) + gb_ref[...]
            log_sig = jnp.minimum(z, 0.0) - jnp.log(1.0 + jnp.exp(-jnp.abs(z)))
            la_ref[rows, :] = log_sig * (LOG2_E / GLA_GATE_NORMALIZER)

    h = h_scr[...]
    for c0 in range(0, w_ref.shape[1], V7X_MXU_COLS):
        cols = slice(c0, c0 + V7X_MXU_COLS)
        acc = jnp.dot(h, w_ref[:, cols], preferred_element_type=F32)
        gated = acc * _sigmoid(acc) * hg_ref[:, cols]
        proj_ref[:, cols] = jnp.where(j >= gate_tile0, gated, acc).astype(BF16)


def _gla_proj(x, g, w_in, w1, w2, gate_b, head_g, layer):
    t, d = x.shape
    n = w_in.shape[2]
    dk = w2.shape[2]
    rank = w1.shape[2]
    dv = head_g.shape[2]
    tm, tn = TM_PROJ, TN_GLA_PROJ
    gate_tile0 = (n - dv) // tn
    fixed = lambda i, j: (layer, 0, 0)
    return pl.pallas_call(
        functools.partial(_gla_proj_kernel, gate_tile0=gate_tile0),
        grid=(t // tm, n // tn),
        in_specs=[
            pl.BlockSpec((tm, d), lambda i, j: (i, 0)),
            pl.BlockSpec((1, d), lambda i, j: (0, 0)),
            pl.BlockSpec((None, d, tn), lambda i, j: (layer, 0, j)),
            pl.BlockSpec((None, d, rank), fixed),
            pl.BlockSpec((None, rank, dk), fixed),
            pl.BlockSpec((None, 1, dk), fixed),
            pl.BlockSpec((None, 1, tn), lambda i, j: (layer, 0, jnp.maximum(j - gate_tile0, 0))),
        ],
        out_specs=[
            pl.BlockSpec((tm, tn), lambda i, j: (i, j)),
            pl.BlockSpec((tm, dk), lambda i, j: (i, 0)),
        ],
        out_shape=[jax.ShapeDtypeStruct((t, n), BF16), jax.ShapeDtypeStruct((t, dk), F32)],
        scratch_shapes=[pltpu.VMEM((tm, d), BF16)],
        compiler_params=_params("parallel", "arbitrary"),
        name="gla_in_proj",
    )(x, g, w_in, w1, w2, gate_b, head_g)


def _gla_decay_matrix():
    c = GLA_CHUNK
    i = np.arange(c)[:, None]
    t = np.arange(c)[None, :]
    blocks = [t <= i, t > i]
    for b in GLA_LEVELS:
        ref = (i // (2 * b)) * (2 * b) + b - 1
        blocks.append(np.where(i > ref, (t > ref) & (t <= i), (t > i) & (t <= ref)))
    a = np.concatenate(blocks, axis=0).astype(np.float32)
    return np.concatenate([a, a], axis=1)


def _gla_kernel(qk_ref, v_ref, rg_ref, la_ref, amat_ref, x_ref, wo_ref, o_ref, s_scr, og_scr):
    c = GLA_CHUNK
    lt = qk_ref.shape[0]
    heads = s_scr.shape[0]
    dk, dv = s_scr.shape[1], s_scr.shape[2]
    k_base = heads * dk

    @pl.when(pl.program_id(1) == 0)
    def _():
        s_scr[...] = jnp.zeros_like(s_scr)

    ri = lax.broadcasted_iota(jnp.int32, (c, c), 0)
    ci = lax.broadcasted_iota(jnp.int32, (c, c), 1)
    masks = [ri == ci]
    for b in GLA_LEVELS:
        masks.append((ri // (2 * b) == ci // (2 * b)) & ((ri // b) % 2 == 1) & ((ci // b) % 2 == 0))
    amat = amat_ref[...]
    nt = (((1,), (1,)), ((), ()))

    def chunk(ic, carry):
        rows = pl.ds(pl.multiple_of(ic * c, c), c)
        qcols = [slice(h * dk, (h + 1) * dk) for h in range(heads)]
        kcols = [slice(k_base + h * dk, k_base + (h + 1) * dk) for h in range(heads)]
        vcols = [slice(h * dv, (h + 1) * dv) for h in range(heads)]
        las = [la_ref[rows, qcols[h]] for h in range(heads)]
        e_rem, es = [], []
        for h in range(heads):
            la_hi = las[h].astype(BF16)
            la_lo = (las[h] - la_hi.astype(F32)).astype(BF16)
            expo = jnp.dot(amat, jnp.concatenate([la_hi, la_lo], axis=0), preferred_element_type=F32)
            e = jnp.exp2(expo)
            e_rem.append(e[c:2 * c])
            es.append(e.astype(BF16))
        qs = [qk_ref[rows, qcols[h]] for h in range(heads)]
        ks = [qk_ref[rows, kcols[h]] for h in range(heads)]
        scores = [jnp.where(masks[0], lax.dot_general(qs[h], ks[h], nt, preferred_element_type=F32), 0.0)
                  for h in range(heads)]
        for lvl in range(len(GLA_LEVELS)):
            for h in range(heads):
                eb = es[h][(2 + lvl) * c:(3 + lvl) * c]
                sb = lax.dot_general(qs[h] * eb, ks[h] * eb, nt, preferred_element_type=F32)
                scores[h] = scores[h] + jnp.where(masks[1 + lvl], sb, 0.0)
        for h in range(heads):
            v = v_ref[rows, vcols[h]]
            state = s_scr[h]
            o = jnp.dot(qs[h] * es[h][0:c], state.astype(BF16), preferred_element_type=F32)
            o = o + jnp.dot(scores[h].astype(BF16), v, preferred_element_type=F32)
            k_rem_t = (ks[h].astype(F32) * e_rem[h]).T.astype(BF16)
            e_last = jnp.exp2(jnp.sum(las[h].T, axis=1, keepdims=True))
            s_scr[h] = state * e_last + jnp.dot(k_rem_t, v, preferred_element_type=F32)
            ms = jnp.mean(o * o, axis=-1, keepdims=True)
            gate = rg_ref[rows, vcols[h]].astype(F32)
            og_scr[rows, vcols[h]] = (o * lax.rsqrt(ms + EPS * dk) * gate).astype(BF16)
        return carry

    lax.fori_loop(0, lt // c, chunk, 0)
    o_ref[...] = x_ref[...] + jnp.dot(og_scr[...], wo_ref[...], preferred_element_type=F32)


def _gla(proj, log_a, x, w_out, layer, batch, seq):
    t, d = x.shape
    dk_total = log_a.shape[1]
    dv_total = (proj.shape[1] - 2 * dk_total) // 2
    assert 2 * dk_total == dv_total
    lt = GLA_LT
    steps = seq // lt
    amat = jnp.asarray(_gla_decay_matrix(), BF16)
    row = lambda b, s: b * steps + s
    return pl.pallas_call(
        _gla_kernel,
        grid=(batch, steps),
        in_specs=[
            pl.BlockSpec((lt, 2 * dk_total), lambda b, s: (row(b, s), 0)),
            pl.BlockSpec((lt, dv_total), lambda b, s: (row(b, s), 1)),
            pl.BlockSpec((lt, dv_total), lambda b, s: (row(b, s), 2)),
            pl.BlockSpec((lt, dk_total), lambda b, s: (row(b, s), 0)),
            pl.BlockSpec(amat.shape, lambda b, s: (0, 0)),
            pl.BlockSpec((lt, d), lambda b, s: (row(b, s), 0)),
            pl.BlockSpec((None, dv_total, d), lambda b, s: (layer, 0, 0), pipeline_mode=pl.Buffered(1)),
        ],
        out_specs=pl.BlockSpec((lt, d), lambda b, s: (row(b, s), 0)),
        out_shape=jax.ShapeDtypeStruct((t, d), F32),
        scratch_shapes=[
            pltpu.VMEM((GLA_HEADS, dk_total // GLA_HEADS, dv_total // GLA_HEADS), F32),
            pltpu.VMEM((lt, dv_total), BF16),
        ],
        compiler_params=_params("parallel", "arbitrary"),
        name="gla_chunked",
    )(proj, proj, proj, log_a, amat, x, w_out)


def kernel(x, norm_mix_g, norm_ffn_g, conv_w_in, conv_b_in, conv_w_dw, conv_b_dw, conv_ln_g, conv_ln_b,
           conv_w_out, conv_b_out, gla_w_in, gla_gate_w1, gla_gate_w2, gla_gate_b, gla_head_norm_g,
           gla_w_out, ffn_w_gate_up, ffn_w_down, final_norm_g):
    batch, seq, d = x.shape
    depth = norm_mix_g.shape[0]
    assert all(seq % tile == 0 for tile in (TM_PROJ, TM_FFN, TM_CONV, GLA_LT))
    rows3d = lambda a: a.reshape(a.shape[0], 1, -1)
    conv_w_in, conv_w_out = conv_w_in.astype(BF16), conv_w_out.astype(BF16)
    gla_w_in, gla_w_out = gla_w_in.astype(BF16), gla_w_out.astype(BF16)
    gla_gate_w1, gla_gate_w2 = gla_gate_w1.astype(BF16), gla_gate_w2.astype(BF16)
    ffn_w_gate_up, ffn_w_down = ffn_w_gate_up.astype(BF16), ffn_w_down.astype(BF16)
    norm_ffn_g3 = rows3d(norm_ffn_g)
    final_g = final_norm_g.reshape(1, -1)
    h = x.reshape(batch * seq, d)
    for i in range(depth):
        j = i // 2
        g_mix = norm_mix_g[i].reshape(1, -1)
        if i % 2 == 0:
            h = _conv_module(h, g_mix, conv_w_in, rows3d(conv_b_in), conv_w_dw, rows3d(conv_b_dw),
                             rows3d(conv_ln_g), rows3d(conv_ln_b), conv_w_out, rows3d(conv_b_out), j, seq)
        else:
            proj, log_a = _gla_proj(h, g_mix, gla_w_in, gla_gate_w1, gla_gate_w2, rows3d(gla_gate_b),
                                    rows3d(gla_head_norm_g), j)
            h = _gla(proj, log_a, h, gla_w_out, j, batch, seq)
        h = _ffn(h, norm_ffn_g3, ffn_w_gate_up, ffn_w_down, final_g, i, final_norm=(i == depth - 1))
    return h.reshape(batch, seq, d)
```

```python
import functools

import jax
import jax.numpy as jnp
import numpy as np
from jax import lax
from jax.experimental import pallas as pl
from jax.experimental.pallas import tpu as pltpu

F32 = jnp.float32
BF16 = jnp.bfloat16

EPS = 1e-6
CONV_WIDTH = 31
GLA_HEADS = 4
GLA_GATE_NORMALIZER = 16.0
LOG2_E = 1.4426950408889634

V7X_LANES = 128
V7X_SUBLANES = 8
V7X_MXU_COLS = 256
V7X_VMEM_BYTES = 64 * 1024 * 1024

VMEM_LIMIT = V7X_VMEM_BYTES - 8 * 1024 * 1024
TM_PROJ = 1024
TM_FFN = 1024
FFN_ROWS = 256
TN_PROJ = 512
TN_GLA_PROJ = 1024
GATE_ROWS = 256
TM_CONV = 256
CONV_CB = 256
CONV_HALO = 32
CONV_ROWS = 128
GLA_CHUNK = 64
GLA_LT = 512
GLA_LEVELS = (32, 16, 8, 4, 2, 1)


def _params(*sem):
    return pltpu.CompilerParams(dimension_semantics=sem, vmem_limit_bytes=VMEM_LIMIT)


def _rms_norm_bf16(x, g):
    ms = jnp.mean(x * x, axis=-1, keepdims=True)
    return (x * lax.rsqrt(ms + EPS) * g).astype(BF16)


def _sigmoid(x):
    return 1.0 / (1.0 + jnp.exp(-x))


def _conv_module_kernel(x_ref, xp_ref, g_ref, win_ref, bin_ref, wdw_ref, bdw_ref, lng_ref, lnb_ref,
                        wo_ref, bo_ref, o_ref, h_scr, ext_scr, sh_scr, c_scr, s_scr, *, tiles_per_seq):
    i = pl.program_id(0)
    tm, d = x_ref.shape
    sub, lanes_w = V7X_SUBLANES, V7X_LANES
    nblk = d // CONV_CB
    shift = CONV_HALO - (CONV_WIDTH - 1)
    groups = (tm + CONV_HALO) // sub
    sub_idx = lax.broadcasted_iota(jnp.int32, (groups - 1, sub, lanes_w), 1)

    @pl.when(i == 0)
    def _():
        s_scr[...] = jnp.zeros_like(s_scr)
        ext_scr[tm:, :] = jnp.zeros((CONV_HALO, d), F32)

    first = (i % tiles_per_seq) == 0
    ext_scr[0:CONV_HALO, :] = jnp.where(first, 0.0, ext_scr[tm:, :])
    h_scr[...] = _rms_norm_bf16(x_ref[...], g_ref[...])

    def in_proj(cb):
        cols = pl.ds(cb * CONV_CB, CONV_CB)
        gcols = pl.ds(d + cb * CONV_CB, CONV_CB)
        h = h_scr[...]
        a = jnp.dot(h, win_ref[:, cols], preferred_element_type=F32) + bin_ref[:, cols]
        g = jnp.dot(h, win_ref[:, gcols], preferred_element_type=F32) + bin_ref[:, gcols]
        ext_scr[CONV_HALO:, cols] = a * _sigmoid(g)

    def conv(cb):
        for part in range(CONV_CB // lanes_w):
            lanes = pl.ds(cb * CONV_CB + part * lanes_w, lanes_w)
            x3 = ext_scr[:, lanes].reshape(groups, sub, lanes_w)
            for r in range(1, sub):
                rot = pltpu.roll(x3, sub - r, axis=1)
                y = jnp.where(sub_idx < sub - r, rot[:-1], rot[1:])
                sh_scr[part, r, 0:(groups - 1) * sub, :] = y.reshape((groups - 1) * sub, lanes_w)
            for r0 in range(0, tm, CONV_ROWS):
                acc = None
                for k in range(CONV_WIDTH):
                    a, r = divmod(shift + k, sub)
                    rows = pl.ds(r0 + a * sub, CONV_ROWS)
                    seg = ext_scr[rows, lanes] if r == 0 else sh_scr[part, r, rows, :]
                    term = seg * wdw_ref[pl.ds(k, 1), lanes]
                    acc = term if acc is None else acc + term
                c_scr[pl.ds(r0, CONV_ROWS), lanes] = acc + bdw_ref[:, lanes]

    def out_proj(cb):
        cols = pl.ds(cb * CONV_CB, CONV_CB)
        o_ref[:, cols] = (xp_ref[:, cols] + bo_ref[:, cols]
                          + jnp.dot(s_scr[...], wo_ref[:, cols], preferred_element_type=F32))

    in_proj(0)
    for cb in range(nblk):
        if cb + 1 < nblk:
            in_proj(cb + 1)
        conv(cb)
        out_proj(cb)

    c = c_scr[...]
    mu = jnp.mean(c, axis=-1, keepdims=True)
    cc = c - mu
    var = jnp.mean(cc * cc, axis=-1, keepdims=True)
    y = cc * lax.rsqrt(var + EPS) * lng_ref[...] + lnb_ref[...]
    s_scr[...] = (y * _sigmoid(y)).astype(BF16)


def _conv_module(x, g, w_in, b_in, w_dw, b_dw, ln_g, ln_b, w_out, b_out, layer, seq):
    t, d = x.shape
    tm = TM_CONV
    n = t // tm
    cur = lambda i: (jnp.minimum(i, n - 1), 0)
    prev = lambda i: (jnp.maximum(i - 1, 0), 0)
    fixed = lambda i: (layer, 0, 0)
    once = pl.Buffered(1)
    return pl.pallas_call(
        functools.partial(_conv_module_kernel, tiles_per_seq=seq // tm),
        grid=(n + 1,),
        in_specs=[
            pl.BlockSpec((tm, d), cur),
            pl.BlockSpec((tm, d), prev),
            pl.BlockSpec((1, d), lambda i: (0, 0)),
            pl.BlockSpec((None, d, 2 * d), fixed, pipeline_mode=once),
            pl.BlockSpec((None, 1, 2 * d), fixed),
            pl.BlockSpec((None, CONV_WIDTH, d), fixed),
            pl.BlockSpec((None, 1, d), fixed),
            pl.BlockSpec((None, 1, d), fixed),
            pl.BlockSpec((None, 1, d), fixed),
            pl.BlockSpec((None, d, d), fixed, pipeline_mode=once),
            pl.BlockSpec((None, 1, d), fixed),
        ],
        out_specs=pl.BlockSpec((tm, d), prev),
        out_shape=jax.ShapeDtypeStruct((t, d), F32),
        scratch_shapes=[
            pltpu.VMEM((tm, d), BF16),
            pltpu.VMEM((tm + CONV_HALO, d), F32),
            pltpu.VMEM((CONV_CB // V7X_LANES, V7X_SUBLANES, tm + CONV_HALO, V7X_LANES), F32),
            pltpu.VMEM((tm, d), F32),
            pltpu.VMEM((tm, d), BF16),
        ],
        compiler_params=_params("arbitrary"),
        name="conv_module",
    )(x, x, g, w_in, b_in, w_dw, b_dw, ln_g, ln_b, w_out, b_out)


def _ffn_kernel(x_hbm, g_ref, wa_ref, wg_ref, wd_ref, fg_ref, o_ref, x_buf, h_scr, x_sem, *, final_norm):
    i, j = pl.program_id(0), pl.program_id(1)
    tm = x_buf.shape[0]

    def x_copy(tile):
        rows = pl.ds(pl.multiple_of(tile * tm, tm), tm)
        return pltpu.make_async_copy(x_hbm.at[rows], x_buf, x_sem)

    def partial_out(h):
        a = jnp.dot(h, wa_ref[...], preferred_element_type=F32)
        g = jnp.dot(h, wg_ref[...], preferred_element_type=F32)
        p = (a * _sigmoid(a) * g).astype(BF16)
        return jnp.dot(p, wd_ref[...], preferred_element_type=F32)

    @pl.when(j == 0)
    def _():
        @pl.when(i == 0)
        def _():
            x_copy(i).start()

        x_copy(i).wait()
        for r0 in range(0, tm, FFN_ROWS):
            rows = slice(r0, r0 + FFN_ROWS)
            x = x_buf[rows, :]
            h = _rms_norm_bf16(x, g_ref[...])
            h_scr[rows, :] = h
            o_ref[rows, :] = x + partial_out(h)

    @pl.when((j == 1) & (i + 1 < pl.num_programs(0)))
    def _():
        x_copy(i + 1).start()

    @pl.when(j > 0)
    def _():
        o_ref[...] += partial_out(h_scr[...])

    if final_norm:
        @pl.when(j == pl.num_programs(1) - 1)
        def _():
            y = o_ref[...]
            ms = jnp.mean(y * y, axis=-1, keepdims=True)
            o_ref[...] = y * lax.rsqrt(ms + EPS) * fg_ref[...]


def _ffn(x, g, w_gate_up, w_down, final_g, layer, final_norm):
    t, d = x.shape
    d_ff = w_down.shape[1]
    tm, tf = TM_FFN, TN_PROJ
    nj = d_ff // tf
    assert nj >= 2
    return pl.pallas_call(
        functools.partial(_ffn_kernel, final_norm=final_norm),
        grid=(t // tm, nj),
        in_specs=[
            pl.BlockSpec(memory_space=pl.ANY),
            pl.BlockSpec((None, 1, d), lambda i, j: (layer, 0, 0)),
            pl.BlockSpec((None, d, tf), lambda i, j: (layer, 0, j)),
            pl.BlockSpec((None, d, tf), lambda i, j: (layer, 0, j + nj)),
            pl.BlockSpec((None, tf, d), lambda i, j: (layer, j, 0)),
            pl.BlockSpec((1, d), lambda i, j: (0, 0)),
        ],
        out_specs=pl.BlockSpec((tm, d), lambda i, j: (i, 0)),
        out_shape=jax.ShapeDtypeStruct((t, d), F32),
        scratch_shapes=[pltpu.VMEM((tm, d), F32), pltpu.VMEM((tm, d), BF16), pltpu.SemaphoreType.DMA(())],
        compiler_params=_params("arbitrary", "arbitrary"),
        name="ffn_swiglu",
    )(x, g, w_gate_up, w_gate_up, w_down, final_g)


def _gla_proj_kernel(x_ref, g_ref, w_ref, w1_ref, w2_ref, gb_ref, hg_ref, proj_ref, la_ref, h_scr, *,
                     gate_tile0):
    j = pl.program_id(1)

    @pl.when(j == 0)
    def _():
        for r0 in range(0, x_ref.shape[0], GATE_ROWS):
            rows = slice(r0, r0 + GATE_ROWS)
            h = _rms_norm_bf16(x_ref[rows, :], g_ref[...])
            h_scr[rows, :] = h
            low = jnp.dot(h, w1_ref[...], preferred_element_type=F32).astype(BF16)
            z = jnp.dot(low, w2_ref[...], preferred_element_type=F32) + gb_ref[...]
            log_sig = jnp.minimum(z, 0.0) - jnp.log(1.0 + jnp.exp(-jnp.abs(z)))
            la_ref[rows, :] = log_sig * (LOG2_E / GLA_GATE_NORMALIZER)

    h = h_scr[...]
    for c0 in range(0, w_ref.shape[1], V7X_MXU_COLS):
        cols = slice(c0, c0 + V7X_MXU_COLS)
        acc = jnp.dot(h, w_ref[:, cols], preferred_element_type=F32)
        gated = acc * _sigmoid(acc) * hg_ref[:, cols]
        proj_ref[:, cols] = jnp.where(j >= gate_tile0, gated, acc).astype(BF16)


def _gla_proj(x, g, w_in, w1, w2, gate_b, head_g, layer):
    t, d = x.shape
    n = w_in.shape[2]
    dk = w2.shape[2]
    rank = w1.shape[2]
    dv = head_g.shape[2]
    tm, tn = TM_PROJ, TN_GLA_PROJ
    gate_tile0 = (n - dv) // tn
    fixed = lambda i, j: (layer, 0, 0)
    return pl.pallas_call(
        functools.partial(_gla_proj_kernel, gate_tile0=gate_tile0),
        grid=(t // tm, n // tn),
        in_specs=[
            pl.BlockSpec((tm, d), lambda i, j: (i, 0)),
            pl.BlockSpec((1, d), lambda i, j: (0, 0)),
            pl.BlockSpec((None, d, tn), lambda i, j: (layer, 0, j)),
            pl.BlockSpec((None, d, rank), fixed),
            pl.BlockSpec((None, rank, dk), fixed),
            pl.BlockSpec((None, 1, dk), fixed),
            pl.BlockSpec((None, 1, tn), lambda i, j: (layer, 0, jnp.maximum(j - gate_tile0, 0))),
        ],
        out_specs=[
            pl.BlockSpec((tm, tn), lambda i, j: (i, j)),
            pl.BlockSpec((tm, dk), lambda i, j: (i, 0)),
        ],
        out_shape=[jax.ShapeDtypeStruct((t, n), BF16), jax.ShapeDtypeStruct((t, dk), F32)],
        scratch_shapes=[pltpu.VMEM((tm, d), BF16)],
        compiler_params=_params("parallel", "arbitrary"),
        name="gla_in_proj",
    )(x, g, w_in, w1, w2, gate_b, head_g)


def _gla_decay_matrix():
    c = GLA_CHUNK
    i = np.arange(c)[:, None]
    t = np.arange(c)[None, :]
    blocks = [t <= i, t > i]
    for b in GLA_LEVELS:
        ref = (i // (2 * b)) * (2 * b) + b - 1
        blocks.append(np.where(i > ref, (t > ref) & (t <= i), (t > i) & (t <= ref)))
    a = np.concatenate(blocks, axis=0).astype(np.float32)
    return np.concatenate([a, a], axis=1)


def _gla_kernel(qk_ref, v_ref, rg_ref, la_ref, amat_ref, x_ref, wo_ref, o_ref, s_scr, og_scr):
    c = GLA_CHUNK
    lt = qk_ref.shape[0]
    heads = s_scr.shape[0]
    dk, dv = s_scr.shape[1], s_scr.shape[2]
    k_base = heads * dk

    @pl.when(pl.program_id(1) == 0)
    def _():
        s_scr[...] = jnp.zeros_like(s_scr)

    ri = lax.broadcasted_iota(jnp.int32, (c, c), 0)
    ci = lax.broadcasted_iota(jnp.int32, (c, c), 1)
    masks = [ri == ci]
    for b in GLA_LEVELS:
        masks.append((ri // (2 * b) == ci // (2 * b)) & ((ri // b) % 2 == 1) & ((ci // b) % 2 == 0))
    amat = amat_ref[...]
    nt = (((1,), (1,)), ((), ()))

    def chunk(ic, carry):
        rows = pl.ds(pl.multiple_of(ic * c, c), c)
        qcols = [slice(h * dk, (h + 1) * dk) for h in range(heads)]
        kcols = [slice(k_base + h * dk, k_base + (h + 1) * dk) for h in range(heads)]
        vcols = [slice(h * dv, (h + 1) * dv) for h in range(heads)]
        las = [la_ref[rows, qcols[h]] for h in range(heads)]
        e_rem, es = [], []
        for h in range(heads):
            la_hi = las[h].astype(BF16)
            la_lo = (las[h] - la_hi.astype(F32)).astype(BF16)
            expo = jnp.dot(amat, jnp.concatenate([la_hi, la_lo], axis=0), preferred_element_type=F32)
            e = jnp.exp2(expo)
            e_rem.append(e[c:2 * c])
            es.append(e.astype(BF16))
        qs = [qk_ref[rows, qcols[h]] for h in range(heads)]
        ks = [qk_ref[rows, kcols[h]] for h in range(heads)]
        scores = [jnp.where(masks[0], lax.dot_general(qs[h], ks[h], nt, preferred_element_type=F32), 0.0)
                  for h in range(heads)]
        for lvl in range(len(GLA_LEVELS)):
            for h in range(heads):
                eb = es[h][(2 + lvl) * c:(3 + lvl) * c]
                sb = lax.dot_general(qs[h] * eb, ks[h] * eb, nt, preferred_element_type=F32)
                scores[h] = scores[h] + jnp.where(masks[1 + lvl], sb, 0.0)
        for h in range(heads):
            v = v_ref[rows, vcols[h]]
            state = s_scr[h]
            o = jnp.dot(qs[h] * es[h][0:c], state.astype(BF16), preferred_element_type=F32)
            o = o + jnp.dot(scores[h].astype(BF16), v, preferred_element_type=F32)
            k_rem_t = (ks[h].astype(F32) * e_rem[h]).T.astype(BF16)
            e_last = jnp.exp2(jnp.sum(las[h].T, axis=1, keepdims=True))
            s_scr[h] = state * e_last + jnp.dot(k_rem_t, v, preferred_element_type=F32)
            ms = jnp.mean(o * o, axis=-1, keepdims=True)
            gate = rg_ref[rows, vcols[h]].astype(F32)
            og_scr[rows, vcols[h]] = (o * lax.rsqrt(ms + EPS * dk) * gate).astype(BF16)
        return carry

    lax.fori_loop(0, lt // c, chunk, 0, unroll=2)
    o_ref[...] = x_ref[...] + jnp.dot(og_scr[...], wo_ref[...], preferred_element_type=F32)


def _gla(proj, log_a, x, w_out, layer, batch, seq):
    t, d = x.shape
    dk_total = log_a.shape[1]
    dv_total = (proj.shape[1] - 2 * dk_total) // 2
    assert 2 * dk_total == dv_total
    lt = GLA_LT
    steps = seq // lt
    amat = jnp.asarray(_gla_decay_matrix(), BF16)
    row = lambda b, s: b * steps + s
    return pl.pallas_call(
        _gla_kernel,
        grid=(batch, steps),
        in_specs=[
            pl.BlockSpec((lt, 2 * dk_total), lambda b, s: (row(b, s), 0)),
            pl.BlockSpec((lt, dv_total), lambda b, s: (row(b, s), 1)),
            pl.BlockSpec((lt, dv_total), lambda b, s: (row(b, s), 2)),
            pl.BlockSpec((lt, dk_total), lambda b, s: (row(b, s), 0)),
            pl.BlockSpec(amat.shape, lambda b, s: (0, 0)),
            pl.BlockSpec((lt, d), lambda b, s: (row(b, s), 0)),
            pl.BlockSpec((None, dv_total, d), lambda b, s: (layer, 0, 0), pipeline_mode=pl.Buffered(1)),
        ],
        out_specs=pl.BlockSpec((lt, d), lambda b, s: (row(b, s), 0)),
        out_shape=jax.ShapeDtypeStruct((t, d), F32),
        scratch_shapes=[
            pltpu.VMEM((GLA_HEADS, dk_total // GLA_HEADS, dv_total // GLA_HEADS), F32),
            pltpu.VMEM((lt, dv_total), BF16),
        ],
        compiler_params=_params("parallel", "arbitrary"),
        name="gla_chunked",
    )(proj, proj, proj, log_a, amat, x, w_out)


def kernel(x, norm_mix_g, norm_ffn_g, conv_w_in, conv_b_in, conv_w_dw, conv_b_dw, conv_ln_g, conv_ln_b,
           conv_w_out, conv_b_out, gla_w_in, gla_gate_w1, gla_gate_w2, gla_gate_b, gla_head_norm_g,
           gla_w_out, ffn_w_gate_up, ffn_w_down, final_norm_g):
    batch, seq, d = x.shape
    depth = norm_mix_g.shape[0]
    assert all(seq % tile == 0 for tile in (TM_PROJ, TM_FFN, TM_CONV, GLA_LT))
    rows3d = lambda a: a.reshape(a.shape[0], 1, -1)
    conv_w_in, conv_w_out = conv_w_in.astype(BF16), conv_w_out.astype(BF16)
    gla_w_in, gla_w_out = gla_w_in.astype(BF16), gla_w_out.astype(BF16)
    gla_gate_w1, gla_gate_w2 = gla_gate_w1.astype(BF16), gla_gate_w2.astype(BF16)
    ffn_w_gate_up, ffn_w_down = ffn_w_gate_up.astype(BF16), ffn_w_down.astype(BF16)
    norm_ffn_g3 = rows3d(norm_ffn_g)
    final_g = final_norm_g.reshape(1, -1)
    h = x.reshape(batch * seq, d)
    for i in range(depth):
        j = i // 2
        g_mix = norm_mix_g[i].reshape(1, -1)
        if i % 2 == 0:
            h = _conv_module(h, g_mix, conv_w_in, rows3d(conv_b_in), conv_w_dw, rows3d(conv_b_dw),
                             rows3d(conv_ln_g), rows3d(conv_ln_b), conv_w_out, rows3d(conv_b_out), j, seq)
        else:
            proj, log_a = _gla_proj(h, g_mix, gla_w_in, gla_gate_w1, gla_gate_w2, rows3d(gla_gate_b),
                                    rows3d(gla_head_norm_g), j)
            h = _gla(proj, log_a, h, gla_w_out, j, batch, seq)
        h = _ffn(h, norm_ffn_g3, ffn_w_gate_up, ffn_w_down, final_g, i, final_norm=(i == depth - 1))
    return h.reshape(batch, seq, d)
```

```python
import functools

import jax
import jax.numpy as jnp
import numpy as np
from jax import lax
from jax.experimental import pallas as pl
from jax.experimental.pallas import tpu as pltpu

F32 = jnp.float32
BF16 = jnp.bfloat16

EPS = 1e-6
CONV_WIDTH = 31
GLA_HEADS = 4
GLA_GATE_NORMALIZER = 16.0
LOG2_E = 1.4426950408889634

V7X_LANES = 128
V7X_SUBLANES = 8
V7X_VMEM_BYTES = 64 * 1024 * 1024

VMEM_LIMIT = V7X_VMEM_BYTES - 8 * 1024 * 1024
TM_PROJ = 1024
TM_FFN = 1024
FFN_ROWS = 256
TN_PROJ = 512
TN_GLA_PROJ = 1024
GATE_ROWS = 256
TM_CONV = 256
CONV_CB = 256
CONV_HALO = 32
CONV_ROWS = 128
GLA_CHUNK = 64
GLA_LT = 512
GLA_LEVELS = (32, 16, 8, 4, 2, 1)


def _params(*sem):
    return pltpu.CompilerParams(dimension_semantics=sem, vmem_limit_bytes=VMEM_LIMIT)


def _rms_norm_bf16(x, g):
    ms = jnp.mean(x * x, axis=-1, keepdims=True)
    return (x * lax.rsqrt(ms + EPS) * g).astype(BF16)


def _sigmoid(x):
    return 1.0 / (1.0 + jnp.exp(-x))


def _conv_module_kernel(x_ref, xp_ref, g_ref, win_ref, bin_ref, wdw_ref, bdw_ref, lng_ref, lnb_ref,
                        wo_ref, bo_ref, o_ref, h_scr, ext_scr, sh_scr, c_scr, s_scr, *, tiles_per_seq):
    i = pl.program_id(0)
    tm, d = x_ref.shape
    sub, lanes_w = V7X_SUBLANES, V7X_LANES
    nblk = d // CONV_CB
    shift = CONV_HALO - (CONV_WIDTH - 1)
    groups = (tm + CONV_HALO) // sub
    sub_idx = lax.broadcasted_iota(jnp.int32, (groups - 1, sub, lanes_w), 1)

    @pl.when(i == 0)
    def _():
        s_scr[...] = jnp.zeros_like(s_scr)
        ext_scr[tm:, :] = jnp.zeros((CONV_HALO, d), F32)

    first = (i % tiles_per_seq) == 0
    ext_scr[0:CONV_HALO, :] = jnp.where(first, 0.0, ext_scr[tm:, :])
    h_scr[...] = _rms_norm_bf16(x_ref[...], g_ref[...])

    def in_proj(cb):
        cols = pl.ds(cb * CONV_CB, CONV_CB)
        gcols = pl.ds(d + cb * CONV_CB, CONV_CB)
        h = h_scr[...]
        a = jnp.dot(h, win_ref[:, cols], preferred_element_type=F32) + bin_ref[:, cols]
        g = jnp.dot(h, win_ref[:, gcols], preferred_element_type=F32) + bin_ref[:, gcols]
        ext_scr[CONV_HALO:, cols] = a * _sigmoid(g)

    def conv(cb):
        for part in range(CONV_CB // lanes_w):
            lanes = pl.ds(cb * CONV_CB + part * lanes_w, lanes_w)
            x3 = ext_scr[:, lanes].reshape(groups, sub, lanes_w)
            for r in range(1, sub):
                rot = pltpu.roll(x3, sub - r, axis=1)
                y = jnp.where(sub_idx < sub - r, rot[:-1], rot[1:])
                sh_scr[part, r, 0:(groups - 1) * sub, :] = y.reshape((groups - 1) * sub, lanes_w)
            for r0 in range(0, tm, CONV_ROWS):
                acc = None
                for k in range(CONV_WIDTH):
                    a, r = divmod(shift + k, sub)
                    rows = pl.ds(r0 + a * sub, CONV_ROWS)
                    seg = ext_scr[rows, lanes] if r == 0 else sh_scr[part, r, rows, :]
                    term = seg * wdw_ref[pl.ds(k, 1), lanes]
                    acc = term if acc is None else acc + term
                c_scr[pl.ds(r0, CONV_ROWS), lanes] = acc + bdw_ref[:, lanes]

    def out_proj(cb):
        cols = pl.ds(cb * CONV_CB, CONV_CB)
        o_ref[:, cols] = (xp_ref[:, cols] + bo_ref[:, cols]
                          + jnp.dot(s_scr[...], wo_ref[:, cols], preferred_element_type=F32))

    in_proj(0)
    for cb in range(nblk):
        if cb + 1 < nblk:
            in_proj(cb + 1)
        conv(cb)
        out_proj(cb)

    c = c_scr[...]
    mu = jnp.mean(c, axis=-1, keepdims=True)
    cc = c - mu
    var = jnp.mean(cc * cc, axis=-1, keepdims=True)
    y = cc * lax.rsqrt(var + EPS) * lng_ref[...] + lnb_ref[...]
    s_scr[...] = (y * _sigmoid(y)).astype(BF16)


def _conv_module(x, g, w_in, b_in, w_dw, b_dw, ln_g, ln_b, w_out, b_out, layer, seq):
    t, d = x.shape
    tm = TM_CONV
    n = t // tm
    cur = lambda i: (jnp.minimum(i, n - 1), 0)
    prev = lambda i: (jnp.maximum(i - 1, 0), 0)
    fixed = lambda i: (layer, 0, 0)
    once = pl.Buffered(1)
    return pl.pallas_call(
        functools.partial(_conv_module_kernel, tiles_per_seq=seq // tm),
        grid=(n + 1,),
        in_specs=[
            pl.BlockSpec((tm, d), cur),
            pl.BlockSpec((tm, d), prev),
            pl.BlockSpec((1, d), lambda i: (0, 0)),
            pl.BlockSpec((None, d, 2 * d), fixed, pipeline_mode=once),
            pl.BlockSpec((None, 1, 2 * d), fixed),
            pl.BlockSpec((None, CONV_WIDTH, d), fixed),
            pl.BlockSpec((None, 1, d), fixed),
            pl.BlockSpec((None, 1, d), fixed),
            pl.BlockSpec((None, 1, d), fixed),
            pl.BlockSpec((None, d, d), fixed, pipeline_mode=once),
            pl.BlockSpec((None, 1, d), fixed),
        ],
        out_specs=pl.BlockSpec((tm, d), prev),
        out_shape=jax.ShapeDtypeStruct((t, d), F32),
        scratch_shapes=[
            pltpu.VMEM((tm, d), BF16),
            pltpu.VMEM((tm + CONV_HALO, d), F32),
            pltpu.VMEM((CONV_CB // V7X_LANES, V7X_SUBLANES, tm + CONV_HALO, V7X_LANES), F32),
            pltpu.VMEM((tm, d), F32),
            pltpu.VMEM((tm, d), BF16),
        ],
        compiler_params=_params("arbitrary"),
        name="conv_module",
    )(x, x, g, w_in, b_in, w_dw, b_dw, ln_g, ln_b, w_out, b_out)


def _ffn_kernel(x_hbm, g_ref, wa_ref, wg_ref, wd_ref, fg_ref, o_ref, x_buf, h_scr, x_sem, *, final_norm):
    i, j = pl.program_id(0), pl.program_id(1)
    tm = x_buf.shape[0]

    def x_copy(tile):
        rows = pl.ds(pl.multiple_of(tile * tm, tm), tm)
        return pltpu.make_async_copy(x_hbm.at[rows], x_buf, x_sem)

    def partial_out(h):
        a = jnp.dot(h, wa_ref[...], preferred_element_type=F32)
        g = jnp.dot(h, wg_ref[...], preferred_element_type=F32)
        p = (a * _sigmoid(a) * g).astype(BF16)
        return jnp.dot(p, wd_ref[...], preferred_element_type=F32)

    @pl.when(j == 0)
    def _():
        @pl.when(i == 0)
        def _():
            x_copy(i).start()

        x_copy(i).wait()
        for r0 in range(0, tm, FFN_ROWS):
            rows = slice(r0, r0 + FFN_ROWS)
            x = x_buf[rows, :]
            h = _rms_norm_bf16(x, g_ref[...])
            h_scr[rows, :] = h
            o_ref[rows, :] = x + partial_out(h)

    @pl.when((j == 1) & (i + 1 < pl.num_programs(0)))
    def _():
        x_copy(i + 1).start()

    @pl.when(j > 0)
    def _():
        o_ref[...] += partial_out(h_scr[...])

    if final_norm:
        @pl.when(j == pl.num_programs(1) - 1)
        def _():
            y = o_ref[...]
            ms = jnp.mean(y * y, axis=-1, keepdims=True)
            o_ref[...] = y * lax.rsqrt(ms + EPS) * fg_ref[...]


def _ffn(x, g, w_gate_up, w_down, final_g, layer, final_norm):
    t, d = x.shape
    d_ff = w_down.shape[1]
    tm, tf = TM_FFN, TN_PROJ
    nj = d_ff // tf
    assert nj >= 2
    return pl.pallas_call(
        functools.partial(_ffn_kernel, final_norm=final_norm),
        grid=(t // tm, nj),
        in_specs=[
            pl.BlockSpec(memory_space=pl.ANY),
            pl.BlockSpec((None, 1, d), lambda i, j: (layer, 0, 0)),
            pl.BlockSpec((None, d, tf), lambda i, j: (layer, 0, j)),
            pl.BlockSpec((None, d, tf), lambda i, j: (layer, 0, j + nj)),
            pl.BlockSpec((None, tf, d), lambda i, j: (layer, j, 0)),
            pl.BlockSpec((1, d), lambda i, j: (0, 0)),
        ],
        out_specs=pl.BlockSpec((tm, d), lambda i, j: (i, 0)),
        out_shape=jax.ShapeDtypeStruct((t, d), F32),
        scratch_shapes=[pltpu.VMEM((tm, d), F32), pltpu.VMEM((tm, d), BF16), pltpu.SemaphoreType.DMA(())],
        compiler_params=_params("arbitrary", "arbitrary"),
        name="ffn_swiglu",
    )(x, g, w_gate_up, w_gate_up, w_down, final_g)


def _gla_proj_kernel(x_ref, g_ref, w_ref, w1_ref, w2_ref, gb_ref, proj_ref, la_ref, h_scr):
    @pl.when(pl.program_id(1) == 0)
    def _():
        for r0 in range(0, x_ref.shape[0], GATE_ROWS):
            rows = slice(r0, r0 + GATE_ROWS)
            h = _rms_norm_bf16(x_ref[rows, :], g_ref[...])
            h_scr[rows, :] = h
            low = jnp.dot(h, w1_ref[...], preferred_element_type=F32).astype(BF16)
            z = jnp.dot(low, w2_ref[...], preferred_element_type=F32) + gb_ref[...]
            log_sig = jnp.minimum(z, 0.0) - jnp.log(1.0 + jnp.exp(-jnp.abs(z)))
            la_ref[rows, :] = log_sig * (LOG2_E / GLA_GATE_NORMALIZER)

    proj_ref[...] = jnp.dot(h_scr[...], w_ref[...], preferred_element_type=F32).astype(BF16)


def _gla_proj(x, g, w_in, w1, w2, gate_b, layer):
    t, d = x.shape
    n = w_in.shape[2]
    dk = w2.shape[2]
    rank = w1.shape[2]
    tm, tn = TM_PROJ, TN_GLA_PROJ
    fixed = lambda i, j: (layer, 0, 0)
    return pl.pallas_call(
        _gla_proj_kernel,
        grid=(t // tm, n // tn),
        in_specs=[
            pl.BlockSpec((tm, d), lambda i, j: (i, 0)),
            pl.BlockSpec((1, d), lambda i, j: (0, 0)),
            pl.BlockSpec((None, d, tn), lambda i, j: (layer, 0, j)),
            pl.BlockSpec((None, d, rank), fixed),
            pl.BlockSpec((None, rank, dk), fixed),
            pl.BlockSpec((None, 1, dk), fixed),
        ],
        out_specs=[
            pl.BlockSpec((tm, tn), lambda i, j: (i, j)),
            pl.BlockSpec((tm, dk), lambda i, j: (i, 0)),
        ],
        out_shape=[jax.ShapeDtypeStruct((t, n), BF16), jax.ShapeDtypeStruct((t, dk), F32)],
        scratch_shapes=[pltpu.VMEM((tm, d), BF16)],
        compiler_params=_params("parallel", "arbitrary"),
        name="gla_in_proj",
    )(x, g, w_in, w1, w2, gate_b)


def _gla_decay_matrix():
    c = GLA_CHUNK
    i = np.arange(c)[:, None]
    t = np.arange(c)[None, :]
    blocks = [t <= i, t > i]
    for b in GLA_LEVELS:
        ref = (i // (2 * b)) * (2 * b) + b - 1
        blocks.append(np.where(i > ref, (t > ref) & (t <= i), (t > i) & (t <= ref)))
    a = np.concatenate(blocks, axis=0).astype(np.float32)
    return np.concatenate([a, a], axis=1)


def _gla_kernel(qk_ref, v_ref, r_ref, la_ref, amat_ref, hg_ref, x_ref, wo_ref, o_ref, s_scr, og_scr):
    c = GLA_CHUNK
    lt = qk_ref.shape[0]
    heads = s_scr.shape[0]
    dk, dv = s_scr.shape[1], s_scr.shape[2]
    k_base = heads * dk

    @pl.when(pl.program_id(1) == 0)
    def _():
        s_scr[...] = jnp.zeros_like(s_scr)

    ri = lax.broadcasted_iota(jnp.int32, (c, c), 0)
    ci = lax.broadcasted_iota(jnp.int32, (c, c), 1)
    masks = [ri == ci]
    for b in GLA_LEVELS:
        masks.append((ri // (2 * b) == ci // (2 * b)) & ((ri // b) % 2 == 1) & ((ci // b) % 2 == 0))
    amat = amat_ref[...]
    nt = (((1,), (1,)), ((), ()))

    def chunk(ic, carry):
        rows = pl.ds(pl.multiple_of(ic * c, c), c)
        qcols = [slice(h * dk, (h + 1) * dk) for h in range(heads)]
        kcols = [slice(k_base + h * dk, k_base + (h + 1) * dk) for h in range(heads)]
        vcols = [slice(h * dv, (h + 1) * dv) for h in range(heads)]
        las = [la_ref[rows, qcols[h]] for h in range(heads)]
        e_rem, es = [], []
        for h in range(heads):
            la_hi = las[h].astype(BF16)
            la_lo = (las[h] - la_hi.astype(F32)).astype(BF16)
            expo = jnp.dot(amat, jnp.concatenate([la_hi, la_lo], axis=0), preferred_element_type=F32)
            e = jnp.exp2(expo)
            e_rem.append(e[c:2 * c])
            es.append(e.astype(BF16))
        qs = [qk_ref[rows, qcols[h]] for h in range(heads)]
        ks = [qk_ref[rows, kcols[h]] for h in range(heads)]
        scores = [jnp.where(masks[0], lax.dot_general(qs[h], ks[h], nt, preferred_element_type=F32), 0.0)
                  for h in range(heads)]
        for lvl in range(len(GLA_LEVELS)):
            for h in range(heads):
                eb = es[h][(2 + lvl) * c:(3 + lvl) * c]
                sb = lax.dot_general(qs[h] * eb, ks[h] * eb, nt, preferred_element_type=F32)
                scores[h] = scores[h] + jnp.where(masks[1 + lvl], sb, 0.0)
        for h in range(heads):
            v = v_ref[rows, vcols[h]]
            state = s_scr[h]
            o = jnp.dot(qs[h] * es[h][0:c], state.astype(BF16), preferred_element_type=F32)
            o = o + jnp.dot(scores[h].astype(BF16), v, preferred_element_type=F32)
            k_rem_t = (ks[h].astype(F32) * e_rem[h]).T.astype(BF16)
            e_last = jnp.exp2(jnp.sum(las[h].T, axis=1, keepdims=True))
            s_scr[h] = state * e_last + jnp.dot(k_rem_t, v, preferred_element_type=F32)
            ms = jnp.mean(o * o, axis=-1, keepdims=True)
            r = r_ref[rows, vcols[h]].astype(F32)
            gate = r * _sigmoid(r) * hg_ref[:, vcols[h]]
            og_scr[rows, vcols[h]] = (o * lax.rsqrt(ms + EPS * dk) * gate).astype(BF16)
        return carry

    lax.fori_loop(0, lt // c, chunk, 0, unroll=2)
    o_ref[...] = x_ref[...] + jnp.dot(og_scr[...], wo_ref[...], preferred_element_type=F32)


def _gla(proj, log_a, head_g, x, w_out, layer, batch, seq):
    t, d = x.shape
    dk_total = log_a.shape[1]
    dv_total = (proj.shape[1] - 2 * dk_total) // 2
    assert 2 * dk_total == dv_total
    lt = GLA_LT
    steps = seq // lt
    amat = jnp.asarray(_gla_decay_matrix(), BF16)
    row = lambda b, s: b * steps + s
    return pl.pallas_call(
        _gla_kernel,
        grid=(batch, steps),
        in_specs=[
            pl.BlockSpec((lt, 2 * dk_total), lambda b, s: (row(b, s), 0)),
            pl.BlockSpec((lt, dv_total), lambda b, s: (row(b, s), 1)),
            pl.BlockSpec((lt, dv_total), lambda b, s: (row(b, s), 2)),
            pl.BlockSpec((lt, dk_total), lambda b, s: (row(b, s), 0)),
            pl.BlockSpec(amat.shape, lambda b, s: (0, 0)),
            pl.BlockSpec((None, 1, dv_total), lambda b, s: (layer, 0, 0)),
            pl.BlockSpec((lt, d), lambda b, s: (row(b, s), 0)),
            pl.BlockSpec((None, dv_total, d), lambda b, s: (layer, 0, 0), pipeline_mode=pl.Buffered(1)),
        ],
        out_specs=pl.BlockSpec((lt, d), lambda b, s: (row(b, s), 0)),
        out_shape=jax.ShapeDtypeStruct((t, d), F32),
        scratch_shapes=[
            pltpu.VMEM((GLA_HEADS, dk_total // GLA_HEADS, dv_total // GLA_HEADS), F32),
            pltpu.VMEM((lt, dv_total), BF16),
        ],
        compiler_params=_params("parallel", "arbitrary"),
        name="gla_chunked",
    )(proj, proj, proj, log_a, amat, head_g, x, w_out)


def kernel(x, norm_mix_g, norm_ffn_g, conv_w_in, conv_b_in, conv_w_dw, conv_b_dw, conv_ln_g, conv_ln_b,
           conv_w_out, conv_b_out, gla_w_in, gla_gate_w1, gla_gate_w2, gla_gate_b, gla_head_norm_g,
           gla_w_out, ffn_w_gate_up, ffn_w_down, final_norm_g):
    batch, seq, d = x.shape
    depth = norm_mix_g.shape[0]
    assert all(seq % tile == 0 for tile in (TM_PROJ, TM_FFN, TM_CONV, GLA_LT))
    rows3d = lambda a: a.reshape(a.shape[0], 1, -1)
    conv_w_in, conv_w_out = conv_w_in.astype(BF16), conv_w_out.astype(BF16)
    gla_w_in, gla_w_out = gla_w_in.astype(BF16), gla_w_out.astype(BF16)
    gla_gate_w1, gla_gate_w2 = gla_gate_w1.astype(BF16), gla_gate_w2.astype(BF16)
    ffn_w_gate_up, ffn_w_down = ffn_w_gate_up.astype(BF16), ffn_w_down.astype(BF16)
    norm_ffn_g3 = rows3d(norm_ffn_g)
    final_g = final_norm_g.reshape(1, -1)
    h = x.reshape(batch * seq, d)
    for i in range(depth):
        j = i // 2
        g_mix = norm_mix_g[i].reshape(1, -1)
        if i % 2 == 0:
            h = _conv_module(h, g_mix, conv_w_in, rows3d(conv_b_in), conv_w_dw, rows3d(conv_b_dw),
                             rows3d(conv_ln_g), rows3d(conv_ln_b), conv_w_out, rows3d(conv_b_out), j, seq)
        else:
            proj, log_a = _gla_proj(h, g_mix, gla_w_in, gla_gate_w1, gla_gate_w2, rows3d(gla_gate_b), j)
            h = _gla(proj, log_a, rows3d(gla_head_norm_g), h, gla_w_out, j, batch, seq)
        h = _ffn(h, norm_ffn_g3, ffn_w_gate_up, ffn_w_down, final_g, i, final_norm=(i == depth - 1))
    return h.reshape(batch, seq, d)
```

```python
import functools

import jax
import jax.numpy as jnp
import numpy as np
from jax import lax
from jax.experimental import pallas as pl
from jax.experimental.pallas import tpu as pltpu

F32 = jnp.float32
BF16 = jnp.bfloat16

EPS = 1e-6
CONV_WIDTH = 31
GLA_HEADS = 4
GLA_GATE_NORMALIZER = 16.0
LOG2_E = 1.4426950408889634

V7X_LANES = 128
V7X_SUBLANES = 8
V7X_MXU_COLS = 256
V7X_VMEM_BYTES = 64 * 1024 * 1024

VMEM_LIMIT = V7X_VMEM_BYTES - 8 * 1024 * 1024
TM_PROJ = 1024
TM_FFN = 1024
FFN_ROWS = 256
TN_PROJ = 512
TN_GLA_PROJ = 1024
GATE_ROWS = 256
TM_CONV = 256
CONV_CB = 256
CONV_HALO = 32
CONV_ROWS = 128
GLA_CHUNK = 64
GLA_LT = 512
GLA_LEVELS = (32, 16, 8, 4, 2, 1)


def _params(*sem):
    return pltpu.CompilerParams(dimension_semantics=sem, vmem_limit_bytes=VMEM_LIMIT)


def _rms_norm_bf16(x, g):
    ms = jnp.mean(x * x, axis=-1, keepdims=True)
    return (x * lax.rsqrt(ms + EPS) * g).astype(BF16)


def _sigmoid(x):
    return 1.0 / (1.0 + jnp.exp(-x))


def _conv_module_kernel(x_ref, xp_ref, g_ref, win_ref, bin_ref, wdw_ref, bdw_ref, lng_ref, lnb_ref,
                        wo_ref, bo_ref, o_ref, h_scr, ext_scr, sh_scr, c_scr, s_scr, *, tiles_per_seq):
    i = pl.program_id(0)
    tm, d = x_ref.shape
    sub, lanes_w = V7X_SUBLANES, V7X_LANES
    nblk = d // CONV_CB
    shift = CONV_HALO - (CONV_WIDTH - 1)
    groups = (tm + CONV_HALO) // sub
    sub_idx = lax.broadcasted_iota(jnp.int32, (groups - 1, sub, lanes_w), 1)

    @pl.when(i == 0)
    def _():
        s_scr[...] = jnp.zeros_like(s_scr)
        ext_scr[tm:, :] = jnp.zeros((CONV_HALO, d), F32)

    first = (i % tiles_per_seq) == 0
    ext_scr[0:CONV_HALO, :] = jnp.where(first, 0.0, ext_scr[tm:, :])
    h_scr[...] = _rms_norm_bf16(x_ref[...], g_ref[...])

    def in_proj(cb):
        cols = pl.ds(cb * CONV_CB, CONV_CB)
        gcols = pl.ds(d + cb * CONV_CB, CONV_CB)
        h = h_scr[...]
        a = jnp.dot(h, win_ref[:, cols], preferred_element_type=F32) + bin_ref[:, cols]
        g = jnp.dot(h, win_ref[:, gcols], preferred_element_type=F32) + bin_ref[:, gcols]
        ext_scr[CONV_HALO:, cols] = a * _sigmoid(g)

    def conv(cb):
        for part in range(CONV_CB // lanes_w):
            lanes = pl.ds(cb * CONV_CB + part * lanes_w, lanes_w)
            x3 = ext_scr[:, lanes].reshape(groups, sub, lanes_w)
            for r in range(1, sub):
                rot = pltpu.roll(x3, sub - r, axis=1)
                y = jnp.where(sub_idx < sub - r, rot[:-1], rot[1:])
                sh_scr[part, r, 0:(groups - 1) * sub, :] = y.reshape((groups - 1) * sub, lanes_w)
            for r0 in range(0, tm, CONV_ROWS):
                acc = None
                for k in range(CONV_WIDTH):
                    a, r = divmod(shift + k, sub)
                    rows = pl.ds(r0 + a * sub, CONV_ROWS)
                    seg = ext_scr[rows, lanes] if r == 0 else sh_scr[part, r, rows, :]
                    term = seg * wdw_ref[pl.ds(k, 1), lanes]
                    acc = term if acc is None else acc + term
                c_scr[pl.ds(r0, CONV_ROWS), lanes] = acc + bdw_ref[:, lanes]

    def out_proj(cb):
        cols = pl.ds(cb * CONV_CB, CONV_CB)
        o_ref[:, cols] = (xp_ref[:, cols] + bo_ref[:, cols]
                          + jnp.dot(s_scr[...], wo_ref[:, cols], preferred_element_type=F32))

    in_proj(0)
    for cb in range(nblk):
        if cb + 1 < nblk:
            in_proj(cb + 1)
        conv(cb)
        out_proj(cb)

    c = c_scr[...]
    mu = jnp.mean(c, axis=-1, keepdims=True)
    cc = c - mu
    var = jnp.mean(cc * cc, axis=-1, keepdims=True)
    y = cc * lax.rsqrt(var + EPS) * lng_ref[...] + lnb_ref[...]
    s_scr[...] = (y * _sigmoid(y)).astype(BF16)


def _conv_module(x, g, w_in, b_in, w_dw, b_dw, ln_g, ln_b, w_out, b_out, layer, seq):
    t, d = x.shape
    tm = TM_CONV
    n = t // tm
    cur = lambda i: (jnp.minimum(i, n - 1), 0)
    prev = lambda i: (jnp.maximum(i - 1, 0), 0)
    fixed = lambda i: (layer, 0, 0)
    once = pl.Buffered(1)
    return pl.pallas_call(
        functools.partial(_conv_module_kernel, tiles_per_seq=seq // tm),
        grid=(n + 1,),
        in_specs=[
            pl.BlockSpec((tm, d), cur),
            pl.BlockSpec((tm, d), prev),
            pl.BlockSpec((1, d), lambda i: (0, 0)),
            pl.BlockSpec((None, d, 2 * d), fixed, pipeline_mode=once),
            pl.BlockSpec((None, 1, 2 * d), fixed),
            pl.BlockSpec((None, CONV_WIDTH, d), fixed),
            pl.BlockSpec((None, 1, d), fixed),
            pl.BlockSpec((None, 1, d), fixed),
            pl.BlockSpec((None, 1, d), fixed),
            pl.BlockSpec((None, d, d), fixed, pipeline_mode=once),
            pl.BlockSpec((None, 1, d), fixed),
        ],
        out_specs=pl.BlockSpec((tm, d), prev),
        out_shape=jax.ShapeDtypeStruct((t, d), F32),
        scratch_shapes=[
            pltpu.VMEM((tm, d), BF16),
            pltpu.VMEM((tm + CONV_HALO, d), F32),
            pltpu.VMEM((CONV_CB // V7X_LANES, V7X_SUBLANES, tm + CONV_HALO, V7X_LANES), F32),
            pltpu.VMEM((tm, d), F32),
            pltpu.VMEM((tm, d), BF16),
        ],
        compiler_params=_params("arbitrary"),
        name="conv_module",
    )(x, x, g, w_in, b_in, w_dw, b_dw, ln_g, ln_b, w_out, b_out)


def _ffn_kernel(x_hbm, g_ref, wa_ref, wg_ref, wd_ref, fg_ref, o_ref, x_buf, h_scr, x_sem, *, final_norm):
    i, j = pl.program_id(0), pl.program_id(1)
    tm = x_buf.shape[0]

    def x_copy(tile):
        rows = pl.ds(pl.multiple_of(tile * tm, tm), tm)
        return pltpu.make_async_copy(x_hbm.at[rows], x_buf, x_sem)

    def partial_out(h):
        a = jnp.dot(h, wa_ref[...], preferred_element_type=F32)
        g = jnp.dot(h, wg_ref[...], preferred_element_type=F32)
        p = (a * _sigmoid(a) * g).astype(BF16)
        return jnp.dot(p, wd_ref[...], preferred_element_type=F32)

    @pl.when(j == 0)
    def _():
        @pl.when(i == 0)
        def _():
            x_copy(i).start()

        x_copy(i).wait()
        for r0 in range(0, tm, FFN_ROWS):
            rows = slice(r0, r0 + FFN_ROWS)
            x = x_buf[rows, :]
            h = _rms_norm_bf16(x, g_ref[...])
            h_scr[rows, :] = h
            o_ref[rows, :] = x + partial_out(h)

    @pl.when((j == 1) & (i + 1 < pl.num_programs(0)))
    def _():
        x_copy(i + 1).start()

    @pl.when(j > 0)
    def _():
        o_ref[...] += partial_out(h_scr[...])

    if final_norm:
        @pl.when(j == pl.num_programs(1) - 1)
        def _():
            y = o_ref[...]
            ms = jnp.mean(y * y, axis=-1, keepdims=True)
            o_ref[...] = y * lax.rsqrt(ms + EPS) * fg_ref[...]


def _ffn(x, g, w_gate_up, w_down, final_g, layer, final_norm):
    t, d = x.shape
    d_ff = w_down.shape[1]
    tm, tf = TM_FFN, TN_PROJ
    nj = d_ff // tf
    assert nj >= 2
    return pl.pallas_call(
        functools.partial(_ffn_kernel, final_norm=final_norm),
        grid=(t // tm, nj),
        in_specs=[
            pl.BlockSpec(memory_space=pl.ANY),
            pl.BlockSpec((None, 1, d), lambda i, j: (layer, 0, 0)),
            pl.BlockSpec((None, d, tf), lambda i, j: (layer, 0, j)),
            pl.BlockSpec((None, d, tf), lambda i, j: (layer, 0, j + nj)),
            pl.BlockSpec((None, tf, d), lambda i, j: (layer, j, 0)),
            pl.BlockSpec((1, d), lambda i, j: (0, 0)),
        ],
        out_specs=pl.BlockSpec((tm, d), lambda i, j: (i, 0)),
        out_shape=jax.ShapeDtypeStruct((t, d), F32),
        scratch_shapes=[pltpu.VMEM((tm, d), F32), pltpu.VMEM((tm, d), BF16), pltpu.SemaphoreType.DMA(())],
        compiler_params=_params("arbitrary", "arbitrary"),
        name="ffn_swiglu",
    )(x, g, w_gate_up, w_gate_up, w_down, final_g)


def _gla_proj_kernel(x_ref, g_ref, w_ref, w1_ref, w2_ref, gb_ref, hg_ref, proj_ref, la_ref, h_scr, *,
                     gate_tile0):
    j = pl.program_id(1)

    @pl.when(j == 0)
    def _():
        for r0 in range(0, x_ref.shape[0], GATE_ROWS):
            rows = slice(r0, r0 + GATE_ROWS)
            h = _rms_norm_bf16(x_ref[rows, :], g_ref[...])
            h_scr[rows, :] = h
            low = jnp.dot(h, w1_ref[...], preferred_element_type=F32).astype(BF16)
            z = jnp.dot(low, w2_ref[...], preferred_element_type=F32) + gb_ref[...]
            log_sig = jnp.minimum(z, 0.0) - jnp.log(1.0 + jnp.exp(-jnp.abs(z)))
            la_ref[rows, :] = log_sig * (LOG2_E / GLA_GATE_NORMALIZER)

    h = h_scr[...]
    for c0 in range(0, w_ref.shape[1], V7X_MXU_COLS):
        cols = slice(c0, c0 + V7X_MXU_COLS)
        acc = jnp.dot(h, w_ref[:, cols], preferred_element_type=F32)
        gated = acc * _sigmoid(acc) * hg_ref[:, cols]
        proj_ref[:, cols] = jnp.where(j >= gate_tile0, gated, acc).astype(BF16)


def _gla_proj(x, g, w_in, w1, w2, gate_b, head_g, layer):
    t, d = x.shape
    n = w_in.shape[2]
    dk = w2.shape[2]
    rank = w1.shape[2]
    dv = head_g.shape[2]
    tm, tn = TM_PROJ, TN_GLA_PROJ
    gate_tile0 = (n - dv) // tn
    fixed = lambda i, j: (layer, 0, 0)
    return pl.pallas_call(
        functools.partial(_gla_proj_kernel, gate_tile0=gate_tile0),
        grid=(t // tm, n // tn),
        in_specs=[
            pl.BlockSpec((tm, d), lambda i, j: (i, 0)),
            pl.BlockSpec((1, d), lambda i, j: (0, 0)),
            pl.BlockSpec((None, d, tn), lambda i, j: (layer, 0, j)),
            pl.BlockSpec((None, d, rank), fixed),
            pl.BlockSpec((None, rank, dk), fixed),
            pl.BlockSpec((None, 1, dk), fixed),
            pl.BlockSpec((None, 1, tn), lambda i, j: (layer, 0, jnp.maximum(j - gate_tile0, 0))),
        ],
        out_specs=[
            pl.BlockSpec((tm, tn), lambda i, j: (i, j)),
            pl.BlockSpec((tm, dk), lambda i, j: (i, 0)),
        ],
        out_shape=[jax.ShapeDtypeStruct((t, n), BF16), jax.ShapeDtypeStruct((t, dk), F32)],
        scratch_shapes=[pltpu.VMEM((tm, d), BF16)],
        compiler_params=_params("parallel", "arbitrary"),
        name="gla_in_proj",
    )(x, g, w_in, w1, w2, gate_b, head_g)


def _gla_decay_matrix():
    c = GLA_CHUNK
    i = np.arange(c)[:, None]
    t = np.arange(c)[None, :]
    blocks = [t <= i, t > i]
    for b in GLA_LEVELS:
        ref = (i // (2 * b)) * (2 * b) + b - 1
        blocks.append(np.where(i > ref, (t > ref) & (t <= i), (t > i) & (t <= ref)))
    a = np.concatenate(blocks, axis=0).astype(np.float32)
    return np.concatenate([a, a], axis=1)


def _gla_kernel(qk_ref, v_ref, rg_ref, la_ref, amat_ref, x_ref, wo_ref, o_ref, s_scr, og_scr):
    c = GLA_CHUNK
    lt = qk_ref.shape[0]
    heads = s_scr.shape[0]
    dk, dv = s_scr.shape[1], s_scr.shape[2]
    k_base = heads * dk

    @pl.when(pl.program_id(1) == 0)
    def _():
        s_scr[...] = jnp.zeros_like(s_scr)

    ri = lax.broadcasted_iota(jnp.int32, (c, c), 0)
    ci = lax.broadcasted_iota(jnp.int32, (c, c), 1)
    masks = [ri == ci]
    for b in GLA_LEVELS:
        masks.append((ri // (2 * b) == ci // (2 * b)) & ((ri // b) % 2 == 1) & ((ci // b) % 2 == 0))
    amat = amat_ref[...]
    nt = (((1,), (1,)), ((), ()))

    def chunk(ic, carry):
        rows = pl.ds(pl.multiple_of(ic * c, c), c)
        qcols = [slice(h * dk, (h + 1) * dk) for h in range(heads)]
        kcols = [slice(k_base + h * dk, k_base + (h + 1) * dk) for h in range(heads)]
        vcols = [slice(h * dv, (h + 1) * dv) for h in range(heads)]
        las = [la_ref[rows, qcols[h]] for h in range(heads)]
        e_rem, es = [], []
        for h in range(heads):
            la_hi = las[h].astype(BF16)
            la_lo = (las[h] - la_hi.astype(F32)).astype(BF16)
            expo = jnp.dot(amat, jnp.concatenate([la_hi, la_lo], axis=0), preferred_element_type=F32)
            e = jnp.exp2(expo)
            e_rem.append(e[c:2 * c])
            es.append(e.astype(BF16))
        qs = [qk_ref[rows, qcols[h]] for h in range(heads)]
        ks = [qk_ref[rows, kcols[h]] for h in range(heads)]
        scores = [jnp.where(masks[0], lax.dot_general(qs[h], ks[h], nt, preferred_element_type=F32), 0.0)
                  for h in range(heads)]
        for lvl in range(len(GLA_LEVELS)):
            for h in range(heads):
                eb = es[h][(2 + lvl) * c:(3 + lvl) * c]
                sb = lax.dot_general(qs[h] * eb, ks[h] * eb, nt, preferred_element_type=F32)
                scores[h] = jnp.where(masks[1 + lvl], sb, scores[h])
        for h in range(heads):
            v = v_ref[rows, vcols[h]]
            state = s_scr[h]
            o = jnp.dot(qs[h] * es[h][0:c], state.astype(BF16), preferred_element_type=F32)
            o = o + jnp.dot(scores[h].astype(BF16), v, preferred_element_type=F32)
            k_rem_t = (ks[h].astype(F32) * e_rem[h]).T.astype(BF16)
            e_last = jnp.exp2(jnp.sum(las[h].T, axis=1, keepdims=True))
            s_scr[h] = state * e_last + jnp.dot(k_rem_t, v, preferred_element_type=F32)
            ms = jnp.mean(o * o, axis=-1, keepdims=True)
            gate = rg_ref[rows, vcols[h]].astype(F32)
            og_scr[rows, vcols[h]] = (o * lax.rsqrt(ms + EPS * dk) * gate).astype(BF16)
        return carry

    lax.fori_loop(0, lt // c, chunk, 0, unroll=2)
    o_ref[...] = x_ref[...] + jnp.dot(og_scr[...], wo_ref[...], preferred_element_type=F32)


def _gla(proj, log_a, x, w_out, layer, batch, seq):
    t, d = x.shape
    dk_total = log_a.shape[1]
    dv_total = (proj.shape[1] - 2 * dk_total) // 2
    assert 2 * dk_total == dv_total
    lt = GLA_LT
    steps = seq // lt
    amat = jnp.asarray(_gla_decay_matrix(), BF16)
    row = lambda b, s: b * steps + s
    return pl.pallas_call(
        _gla_kernel,
        grid=(batch, steps),
        in_specs=[
            pl.BlockSpec((lt, 2 * dk_total), lambda b, s: (row(b, s), 0)),
            pl.BlockSpec((lt, dv_total), lambda b, s: (row(b, s), 1)),
            pl.BlockSpec((lt, dv_total), lambda b, s: (row(b, s), 2)),
            pl.BlockSpec((lt, dk_total), lambda b, s: (row(b, s), 0)),
            pl.BlockSpec(amat.shape, lambda b, s: (0, 0)),
            pl.BlockSpec((lt, d), lambda b, s: (row(b, s), 0)),
            pl.BlockSpec((None, dv_total, d), lambda b, s: (layer, 0, 0), pipeline_mode=pl.Buffered(1)),
        ],
        out_specs=pl.BlockSpec((lt, d), lambda b, s: (row(b, s), 0)),
        out_shape=jax.ShapeDtypeStruct((t, d), F32),
        scratch_shapes=[
            pltpu.VMEM((GLA_HEADS, dk_total // GLA_HEADS, dv_total // GLA_HEADS), F32),
            pltpu.VMEM((lt, dv_total), BF16),
        ],
        compiler_params=_params("parallel", "arbitrary"),
        name="gla_chunked",
    )(proj, proj, proj, log_a, amat, x, w_out)


def kernel(x, norm_mix_g, norm_ffn_g, conv_w_in, conv_b_in, conv_w_dw, conv_b_dw, conv_ln_g, conv_ln_b,
           conv_w_out, conv_b_out, gla_w_in, gla_gate_w1, gla_gate_w2, gla_gate_b, gla_head_norm_g,
           gla_w_out, ffn_w_gate_up, ffn_w_down, final_norm_g):
    batch, seq, d = x.shape
    depth = norm_mix_g.shape[0]
    assert all(seq % tile == 0 for tile in (TM_PROJ, TM_FFN, TM_CONV, GLA_LT))
    rows3d = lambda a: a.reshape(a.shape[0], 1, -1)
    conv_w_in, conv_w_out = conv_w_in.astype(BF16), conv_w_out.astype(BF16)
    gla_w_in, gla_w_out = gla_w_in.astype(BF16), gla_w_out.astype(BF16)
    gla_gate_w1, gla_gate_w2 = gla_gate_w1.astype(BF16), gla_gate_w2.astype(BF16)
    ffn_w_gate_up, ffn_w_down = ffn_w_gate_up.astype(BF16), ffn_w_down.astype(BF16)
    norm_ffn_g3 = rows3d(norm_ffn_g)
    final_g = final_norm_g.reshape(1, -1)
    h = x.reshape(batch * seq, d)
    for i in range(depth):
        j = i // 2
        g_mix = norm_mix_g[i].reshape(1, -1)
        if i % 2 == 0:
            h = _conv_module(h, g_mix, conv_w_in, rows3d(conv_b_in), conv_w_dw, rows3d(conv_b_dw),
                             rows3d(conv_ln_g), rows3d(conv_ln_b), conv_w_out, rows3d(conv_b_out), j, seq)
        else:
            proj, log_a = _gla_proj(h, g_mix, gla_w_in, gla_gate_w1, gla_gate_w2, rows3d(gla_gate_b),
                                    rows3d(gla_head_norm_g), j)
            h = _gla(proj, log_a, h, gla_w_out, j, batch, seq)
        h = _ffn(h, norm_ffn_g3, ffn_w_gate_up, ffn_w_down, final_g, i, final_norm=(i == depth - 1))
    return h.reshape(batch, seq, d)
```

```python
import functools

import jax
import jax.numpy as jnp
import numpy as np
from jax import lax
from jax.experimental import pallas as pl
from jax.experimental.pallas import tpu as pltpu

F32 = jnp.float32
BF16 = jnp.bfloat16

EPS = 1e-6
CONV_WIDTH = 31
GLA_HEADS = 4
GLA_GATE_NORMALIZER = 16.0
LOG2_E = 1.4426950408889634

V7X_LANES = 128
V7X_SUBLANES = 8
V7X_MXU_COLS = 256
V7X_VMEM_BYTES = 64 * 1024 * 1024

VMEM_LIMIT = V7X_VMEM_BYTES - 8 * 1024 * 1024
TM_PROJ = 1024
TM_FFN = 1024
FFN_ROWS = 256
TN_PROJ = 512
TN_GLA_PROJ = 1024
GATE_ROWS = 256
TM_CONV = 256
CONV_CB = 256
CONV_HALO = 32
CONV_ROWS = 128
GLA_CHUNK = 64
GLA_LT = 512
GLA_LEVELS = (32, 16, 8, 4, 2, 1)


def _params(*sem):
    return pltpu.CompilerParams(dimension_semantics=sem, vmem_limit_bytes=VMEM_LIMIT)


def _rms_norm_bf16(x, g):
    ms = jnp.mean(x * x, axis=-1, keepdims=True)
    return (x * lax.rsqrt(ms + EPS) * g).astype(BF16)


def _sigmoid(x):
    return 1.0 / (1.0 + jnp.exp(-x))


def _conv_module_kernel(x_ref, xp_ref, g_ref, win_ref, bin_ref, wdw_ref, bdw_ref, lng_ref, lnb_ref,
                        wo_ref, bo_ref, o_ref, h_scr, ext_scr, sh_scr, c_scr, s_scr, *, tiles_per_seq):
    i = pl.program_id(0)
    tm, d = x_ref.shape
    sub, lanes_w = V7X_SUBLANES, V7X_LANES
    nblk = d // CONV_CB
    shift = CONV_HALO - (CONV_WIDTH - 1)
    groups = (tm + CONV_HALO) // sub
    sub_idx = lax.broadcasted_iota(jnp.int32, (groups - 1, sub, lanes_w), 1)

    @pl.when(i == 0)
    def _():
        s_scr[...] = jnp.zeros_like(s_scr)
        ext_scr[tm:, :] = jnp.zeros((CONV_HALO, d), F32)

    first = (i % tiles_per_seq) == 0
    ext_scr[0:CONV_HALO, :] = jnp.where(first, 0.0, ext_scr[tm:, :])
    h_scr[...] = _rms_norm_bf16(x_ref[...], g_ref[...])

    def in_proj(cb):
        cols = pl.ds(cb * CONV_CB, CONV_CB)
        gcols = pl.ds(d + cb * CONV_CB, CONV_CB)
        h = h_scr[...]
        a = jnp.dot(h, win_ref[:, cols], preferred_element_type=F32) + bin_ref[:, cols]
        g = jnp.dot(h, win_ref[:, gcols], preferred_element_type=F32) + bin_ref[:, gcols]
        ext_scr[CONV_HALO:, cols] = a * _sigmoid(g)

    def conv(cb):
        for part in range(CONV_CB // lanes_w):
            lanes = pl.ds(cb * CONV_CB + part * lanes_w, lanes_w)
            x3 = ext_scr[:, lanes].reshape(groups, sub, lanes_w)
            for r in range(1, sub):
                rot = pltpu.roll(x3, sub - r, axis=1)
                y = jnp.where(sub_idx < sub - r, rot[:-1], rot[1:])
                sh_scr[part, r, 0:(groups - 1) * sub, :] = y.reshape((groups - 1) * sub, lanes_w)
            for r0 in range(0, tm, CONV_ROWS):
                acc = None
                for k in range(CONV_WIDTH):
                    a, r = divmod(shift + k, sub)
                    rows = pl.ds(r0 + a * sub, CONV_ROWS)
                    seg = ext_scr[rows, lanes] if r == 0 else sh_scr[part, r, rows, :]
                    term = seg * wdw_ref[pl.ds(k, 1), lanes]
                    acc = term if acc is None else acc + term
                c_scr[pl.ds(r0, CONV_ROWS), lanes] = acc + bdw_ref[:, lanes]

    def out_proj(cb):
        cols = pl.ds(cb * CONV_CB, CONV_CB)
        o_ref[:, cols] = (xp_ref[:, cols] + bo_ref[:, cols]
                          + jnp.dot(s_scr[...], wo_ref[:, cols], preferred_element_type=F32))

    in_proj(0)
    for cb in range(nblk):
        if cb + 1 < nblk:
            in_proj(cb + 1)
        conv(cb)
        out_proj(cb)

    c = c_scr[...]
    mu = jnp.mean(c, axis=-1, keepdims=True)
    cc = c - mu
    var = jnp.mean(cc * cc, axis=-1, keepdims=True)
    y = cc * lax.rsqrt(var + EPS) * lng_ref[...] + lnb_ref[...]
    s_scr[...] = (y * _sigmoid(y)).astype(BF16)


def _conv_module(x, g, w_in, b_in, w_dw, b_dw, ln_g, ln_b, w_out, b_out, layer, seq):
    t, d = x.shape
    tm = TM_CONV
    n = t // tm
    cur = lambda i: (jnp.minimum(i, n - 1), 0)
    prev = lambda i: (jnp.maximum(i - 1, 0), 0)
    fixed = lambda i: (layer, 0, 0)
    once = pl.Buffered(1)
    return pl.pallas_call(
        functools.partial(_conv_module_kernel, tiles_per_seq=seq // tm),
        grid=(n + 1,),
        in_specs=[
            pl.BlockSpec((tm, d), cur),
            pl.BlockSpec((tm, d), prev),
            pl.BlockSpec((1, d), lambda i: (0, 0)),
            pl.BlockSpec((None, d, 2 * d), fixed, pipeline_mode=once),
            pl.BlockSpec((None, 1, 2 * d), fixed),
            pl.BlockSpec((None, CONV_WIDTH, d), fixed),
            pl.BlockSpec((None, 1, d), fixed),
            pl.BlockSpec((None, 1, d), fixed),
            pl.BlockSpec((None, 1, d), fixed),
            pl.BlockSpec((None, d, d), fixed, pipeline_mode=once),
            pl.BlockSpec((None, 1, d), fixed),
        ],
        out_specs=pl.BlockSpec((tm, d), prev),
        out_shape=jax.ShapeDtypeStruct((t, d), F32),
        scratch_shapes=[
            pltpu.VMEM((tm, d), BF16),
            pltpu.VMEM((tm + CONV_HALO, d), F32),
            pltpu.VMEM((CONV_CB // V7X_LANES, V7X_SUBLANES, tm + CONV_HALO, V7X_LANES), F32),
            pltpu.VMEM((tm, d), F32),
            pltpu.VMEM((tm, d), BF16),
        ],
        compiler_params=_params("arbitrary"),
        name="conv_module",
    )(x, x, g, w_in, b_in, w_dw, b_dw, ln_g, ln_b, w_out, b_out)


def _ffn_kernel(x_hbm, g_ref, wa_ref, wg_ref, wd_ref, fg_ref, o_ref, x_buf, h_scr, x_sem, *, final_norm):
    i, j = pl.program_id(0), pl.program_id(1)
    tm = x_buf.shape[0]

    def x_copy(tile):
        rows = pl.ds(pl.multiple_of(tile * tm, tm), tm)
        return pltpu.make_async_copy(x_hbm.at[rows], x_buf, x_sem)

    def partial_out(h):
        out = None
        for c0 in range(0, wa_ref.shape[1], V7X_MXU_COLS):
            cols = slice(c0, c0 + V7X_MXU_COLS)
            a = jnp.dot(h, wa_ref[:, cols], preferred_element_type=F32)
            g = jnp.dot(h, wg_ref[:, cols], preferred_element_type=F32)
            p = (a * _sigmoid(a) * g).astype(BF16)
            part = jnp.dot(p, wd_ref[cols, :], preferred_element_type=F32)
            out = part if out is None else out + part
        return out

    @pl.when(j == 0)
    def _():
        @pl.when(i == 0)
        def _():
            x_copy(i).start()

        x_copy(i).wait()
        for r0 in range(0, tm, FFN_ROWS):
            rows = slice(r0, r0 + FFN_ROWS)
            x = x_buf[rows, :]
            h = _rms_norm_bf16(x, g_ref[...])
            h_scr[rows, :] = h
            o_ref[rows, :] = x + partial_out(h)

    @pl.when((j == 1) & (i + 1 < pl.num_programs(0)))
    def _():
        x_copy(i + 1).start()

    @pl.when(j > 0)
    def _():
        o_ref[...] += partial_out(h_scr[...])

    if final_norm:
        @pl.when(j == pl.num_programs(1) - 1)
        def _():
            y = o_ref[...]
            ms = jnp.mean(y * y, axis=-1, keepdims=True)
            o_ref[...] = y * lax.rsqrt(ms + EPS) * fg_ref[...]


def _ffn(x, g, w_gate_up, w_down, final_g, layer, final_norm):
    t, d = x.shape
    d_ff = w_down.shape[1]
    tm, tf = TM_FFN, TN_PROJ
    nj = d_ff // tf
    assert nj >= 2
    return pl.pallas_call(
        functools.partial(_ffn_kernel, final_norm=final_norm),
        grid=(t // tm, nj),
        in_specs=[
            pl.BlockSpec(memory_space=pl.ANY),
            pl.BlockSpec((None, 1, d), lambda i, j: (layer, 0, 0)),
            pl.BlockSpec((None, d, tf), lambda i, j: (layer, 0, j)),
            pl.BlockSpec((None, d, tf), lambda i, j: (layer, 0, j + nj)),
            pl.BlockSpec((None, tf, d), lambda i, j: (layer, j, 0)),
            pl.BlockSpec((1, d), lambda i, j: (0, 0)),
        ],
        out_specs=pl.BlockSpec((tm, d), lambda i, j: (i, 0)),
        out_shape=jax.ShapeDtypeStruct((t, d), F32),
        scratch_shapes=[pltpu.VMEM((tm, d), F32), pltpu.VMEM((tm, d), BF16), pltpu.SemaphoreType.DMA(())],
        compiler_params=_params("arbitrary", "arbitrary"),
        name="ffn_swiglu",
    )(x, g, w_gate_up, w_gate_up, w_down, final_g)


def _gla_proj_kernel(x_ref, g_ref, w_ref, w1_ref, w2_ref, gb_ref, hg_ref, proj_ref, la_ref, h_scr, *,
                     gate_tile0):
    j = pl.program_id(1)

    @pl.when(j == 0)
    def _():
        for r0 in range(0, x_ref.shape[0], GATE_ROWS):
            rows = slice(r0, r0 + GATE_ROWS)
            h = _rms_norm_bf16(x_ref[rows, :], g_ref[...])
            h_scr[rows, :] = h
            low = jnp.dot(h, w1_ref[...], preferred_element_type=F32).astype(BF16)
            z = jnp.dot(low, w2_ref[...], preferred_element_type=F32) + gb_ref[...]
            log_sig = jnp.minimum(z, 0.0) - jnp.log(1.0 + jnp.exp(-jnp.abs(z)))
            la_ref[rows, :] = log_sig * (LOG2_E / GLA_GATE_NORMALIZER)

    h = h_scr[...]
    for c0 in range(0, w_ref.shape[1], V7X_MXU_COLS):
        cols = slice(c0, c0 + V7X_MXU_COLS)
        acc = jnp.dot(h, w_ref[:, cols], preferred_element_type=F32)
        gated = acc * _sigmoid(acc) * hg_ref[:, cols]
        proj_ref[:, cols] = jnp.where(j >= gate_tile0, gated, acc).astype(BF16)


def _gla_proj(x, g, w_in, w1, w2, gate_b, head_g, layer):
    t, d = x.shape
    n = w_in.shape[2]
    dk = w2.shape[2]
    rank = w1.shape[2]
    dv = head_g.shape[2]
    tm, tn = TM_PROJ, TN_GLA_PROJ
    gate_tile0 = (n - dv) // tn
    fixed = lambda i, j: (layer, 0, 0)
    return pl.pallas_call(
        functools.partial(_gla_proj_kernel, gate_tile0=gate_tile0),
        grid=(t // tm, n // tn),
        in_specs=[
            pl.BlockSpec((tm, d), lambda i, j: (i, 0)),
            pl.BlockSpec((1, d), lambda i, j: (0, 0)),
            pl.BlockSpec((None, d, tn), lambda i, j: (layer, 0, j)),
            pl.BlockSpec((None, d, rank), fixed),
            pl.BlockSpec((None, rank, dk), fixed),
            pl.BlockSpec((None, 1, dk), fixed),
            pl.BlockSpec((None, 1, tn), lambda i, j: (layer, 0, jnp.maximum(j - gate_tile0, 0))),
        ],
        out_specs=[
            pl.BlockSpec((tm, tn), lambda i, j: (i, j)),
            pl.BlockSpec((tm, dk), lambda i, j: (i, 0)),
        ],
        out_shape=[jax.ShapeDtypeStruct((t, n), BF16), jax.ShapeDtypeStruct((t, dk), F32)],
        scratch_shapes=[pltpu.VMEM((tm, d), BF16)],
        compiler_params=_params("parallel", "arbitrary"),
        name="gla_in_proj",
    )(x, g, w_in, w1, w2, gate_b, head_g)


def _gla_decay_matrix():
    c = GLA_CHUNK
    i = np.arange(c)[:, None]
    t = np.arange(c)[None, :]
    blocks = [t <= i, t > i]
    for b in GLA_LEVELS:
        ref = (i // (2 * b)) * (2 * b) + b - 1
        blocks.append(np.where(i > ref, (t > ref) & (t <= i), (t > i) & (t <= ref)))
    a = np.concatenate(blocks, axis=0).astype(np.float32)
    return np.concatenate([a, a], axis=1)


def _gla_kernel(qk_ref, v_ref, rg_ref, la_ref, amat_ref, x_ref, wo_ref, o_ref, s_scr, og_scr):
    c = GLA_CHUNK
    lt = qk_ref.shape[0]
    heads = s_scr.shape[0]
    dk, dv = s_scr.shape[1], s_scr.shape[2]
    k_base = heads * dk

    @pl.when(pl.program_id(1) == 0)
    def _():
        s_scr[...] = jnp.zeros_like(s_scr)

    ri = lax.broadcasted_iota(jnp.int32, (c, c), 0)
    ci = lax.broadcasted_iota(jnp.int32, (c, c), 1)
    masks = [ri == ci]
    for b in GLA_LEVELS:
        masks.append((ri // (2 * b) == ci // (2 * b)) & ((ri // b) % 2 == 1) & ((ci // b) % 2 == 0))
    amat = amat_ref[...]
    nt = (((1,), (1,)), ((), ()))

    def chunk(ic, carry):
        rows = pl.ds(pl.multiple_of(ic * c, c), c)
        qcols = [slice(h * dk, (h + 1) * dk) for h in range(heads)]
        kcols = [slice(k_base + h * dk, k_base + (h + 1) * dk) for h in range(heads)]
        vcols = [slice(h * dv, (h + 1) * dv) for h in range(heads)]
        las = [la_ref[rows, qcols[h]] for h in range(heads)]
        e_rem, es = [], []
        for h in range(heads):
            la_hi = las[h].astype(BF16)
            la_lo = (las[h] - la_hi.astype(F32)).astype(BF16)
            expo = jnp.dot(amat, jnp.concatenate([la_hi, la_lo], axis=0), preferred_element_type=F32)
            e = jnp.exp2(expo)
            e_rem.append(e[c:2 * c])
            es.append(e.astype(BF16))
        qs = [qk_ref[rows, qcols[h]] for h in range(heads)]
        ks = [qk_ref[rows, kcols[h]] for h in range(heads)]
        scores = [jnp.where(masks[0], lax.dot_general(qs[h], ks[h], nt, preferred_element_type=F32), 0.0)
                  for h in range(heads)]
        for lvl in range(len(GLA_LEVELS)):
            for h in range(heads):
                eb = es[h][(2 + lvl) * c:(3 + lvl) * c]
                sb = lax.dot_general(qs[h] * eb, ks[h] * eb, nt, preferred_element_type=F32)
                scores[h] = scores[h] + jnp.where(masks[1 + lvl], sb, 0.0)
        for h in range(heads):
            v = v_ref[rows, vcols[h]]
            state = s_scr[h]
            o = jnp.dot(qs[h] * es[h][0:c], state.astype(BF16), preferred_element_type=F32)
            o = o + jnp.dot(scores[h].astype(BF16), v, preferred_element_type=F32)
            k_rem_t = (ks[h].astype(F32) * e_rem[h]).T.astype(BF16)
            e_last = jnp.exp2(jnp.sum(las[h].T, axis=1, keepdims=True))
            s_scr[h] = state * e_last + jnp.dot(k_rem_t, v, preferred_element_type=F32)
            ms = jnp.mean(o * o, axis=-1, keepdims=True)
            gate = rg_ref[rows, vcols[h]].astype(F32)
            og_scr[rows, vcols[h]] = (o * lax.rsqrt(ms + EPS * dk) * gate).astype(BF16)
        return carry

    lax.fori_loop(0, lt // c, chunk, 0, unroll=2)
    o_ref[...] = x_ref[...] + jnp.dot(og_scr[...], wo_ref[...], preferred_element_type=F32)


def _gla(proj, log_a, x, w_out, layer, batch, seq):
    t, d = x.shape
    dk_total = log_a.shape[1]
    dv_total = (proj.shape[1] - 2 * dk_total) // 2
    assert 2 * dk_total == dv_total
    lt = GLA_LT
    steps = seq // lt
    amat = jnp.asarray(_gla_decay_matrix(), BF16)
    row = lambda b, s: b * steps + s
    return pl.pallas_call(
        _gla_kernel,
        grid=(batch, steps),
        in_specs=[
            pl.BlockSpec((lt, 2 * dk_total), lambda b, s: (row(b, s), 0)),
            pl.BlockSpec((lt, dv_total), lambda b, s: (row(b, s), 1)),
            pl.BlockSpec((lt, dv_total), lambda b, s: (row(b, s), 2)),
            pl.BlockSpec((lt, dk_total), lambda b, s: (row(b, s), 0)),
            pl.BlockSpec(amat.shape, lambda b, s: (0, 0)),
            pl.BlockSpec((lt, d), lambda b, s: (row(b, s), 0)),
            pl.BlockSpec((None, dv_total, d), lambda b, s: (layer, 0, 0), pipeline_mode=pl.Buffered(1)),
        ],
        out_specs=pl.BlockSpec((lt, d), lambda b, s: (row(b, s), 0)),
        out_shape=jax.ShapeDtypeStruct((t, d), F32),
        scratch_shapes=[
            pltpu.VMEM((GLA_HEADS, dk_total // GLA_HEADS, dv_total // GLA_HEADS), F32),
            pltpu.VMEM((lt, dv_total), BF16),
        ],
        compiler_params=_params("parallel", "arbitrary"),
        name="gla_chunked",
    )(proj, proj, proj, log_a, amat, x, w_out)


def kernel(x, norm_mix_g, norm_ffn_g, conv_w_in, conv_b_in, conv_w_dw, conv_b_dw, conv_ln_g, conv_ln_b,
           conv_w_out, conv_b_out, gla_w_in, gla_gate_w1, gla_gate_w2, gla_gate_b, gla_head_norm_g,
           gla_w_out, ffn_w_gate_up, ffn_w_down, final_norm_g):
    batch, seq, d = x.shape
    depth = norm_mix_g.shape[0]
    assert all(seq % tile == 0 for tile in (TM_PROJ, TM_FFN, TM_CONV, GLA_LT))
    rows3d = lambda a: a.reshape(a.shape[0], 1, -1)
    conv_w_in, conv_w_out = conv_w_in.astype(BF16), conv_w_out.astype(BF16)
    gla_w_in, gla_w_out = gla_w_in.astype(BF16), gla_w_out.astype(BF16)
    gla_gate_w1, gla_gate_w2 = gla_gate_w1.astype(BF16), gla_gate_w2.astype(BF16)
    ffn_w_gate_up, ffn_w_down = ffn_w_gate_up.astype(BF16), ffn_w_down.astype(BF16)
    norm_ffn_g3 = rows3d(norm_ffn_g)
    final_g = final_norm_g.reshape(1, -1)
    h = x.reshape(batch * seq, d)
    for i in range(depth):
        j = i // 2
        g_mix = norm_mix_g[i].reshape(1, -1)
        if i % 2 == 0:
            h = _conv_module(h, g_mix, conv_w_in, rows3d(conv_b_in), conv_w_dw, rows3d(conv_b_dw),
                             rows3d(conv_ln_g), rows3d(conv_ln_b), conv_w_out, rows3d(conv_b_out), j, seq)
        else:
            proj, log_a = _gla_proj(h, g_mix, gla_w_in, gla_gate_w1, gla_gate_w2, rows3d(gla_gate_b),
                                    rows3d(gla_head_norm_g), j)
            h = _gla(proj, log_a, h, gla_w_out, j, batch, seq)
        h = _ffn(h, norm_ffn_g3, ffn_w_gate_up, ffn_w_down, final_g, i, final_norm=(i == depth - 1))
    return h.reshape(batch, seq, d)
```

```python
import functools

import jax
import jax.numpy as jnp
import numpy as np
from jax import lax
from jax.experimental import pallas as pl
from jax.experimental.pallas import tpu as pltpu

F32 = jnp.float32
BF16 = jnp.bfloat16

EPS = 1e-6
CONV_WIDTH = 31
GLA_HEADS = 4
GLA_GATE_NORMALIZER = 16.0
LOG2_E = 1.4426950408889634

V7X_LANES = 128
V7X_SUBLANES = 8
V7X_MXU_COLS = 256
V7X_VMEM_BYTES = 64 * 1024 * 1024

VMEM_LIMIT = V7X_VMEM_BYTES - 8 * 1024 * 1024
TM_PROJ = 1024
TM_FFN = 1024
FFN_ROWS = 256
TN_PROJ = 512
TN_GLA_PROJ = 1024
GATE_ROWS = 256
TM_CONV = 256
CONV_CB = 256
CONV_HALO = 32
CONV_ROWS = 128
GLA_CHUNK = 64
GLA_LT = 512
GLA_LEVELS = (32, 16, 8, 4, 2, 1)


def _params(*sem):
    return pltpu.CompilerParams(dimension_semantics=sem, vmem_limit_bytes=VMEM_LIMIT)


def _rms_norm_bf16(x, g):
    ms = jnp.mean(x * x, axis=-1, keepdims=True)
    return (x * lax.rsqrt(ms + EPS) * g).astype(BF16)


def _sigmoid(x):
    return 1.0 / (1.0 + jnp.exp(-x))


def _conv_module_kernel(x_ref, xp_ref, g_ref, win_ref, bin_ref, wdw_ref, bdw_ref, lng_ref, lnb_ref,
                        wo_ref, bo_ref, o_ref, h_scr, ext_scr, sh_scr, c_scr, s_scr, *, tiles_per_seq):
    i = pl.program_id(0)
    tm, d = x_ref.shape
    sub, lanes_w = V7X_SUBLANES, V7X_LANES
    nblk = d // CONV_CB
    shift = CONV_HALO - (CONV_WIDTH - 1)
    groups = (tm + CONV_HALO) // sub
    sub_idx = lax.broadcasted_iota(jnp.int32, (groups - 1, sub, lanes_w), 1)

    @pl.when(i == 0)
    def _():
        s_scr[...] = jnp.zeros_like(s_scr)
        ext_scr[tm:, :] = jnp.zeros((CONV_HALO, d), F32)

    first = (i % tiles_per_seq) == 0
    last = pl.num_programs(0) - 1

    def in_proj(cb):
        cols = pl.ds(cb * CONV_CB, CONV_CB)
        gcols = pl.ds(d + cb * CONV_CB, CONV_CB)
        h = h_scr[...]
        a = jnp.dot(h, win_ref[:, cols], preferred_element_type=F32) + bin_ref[:, cols]
        g = jnp.dot(h, win_ref[:, gcols], preferred_element_type=F32) + bin_ref[:, gcols]
        ext_scr[CONV_HALO:, cols] = a * _sigmoid(g)

    def conv(cb):
        for part in range(CONV_CB // lanes_w):
            lanes = pl.ds(cb * CONV_CB + part * lanes_w, lanes_w)
            x3 = ext_scr[:, lanes].reshape(groups, sub, lanes_w)
            for r in range(1, sub):
                rot = pltpu.roll(x3, sub - r, axis=1)
                y = jnp.where(sub_idx < sub - r, rot[:-1], rot[1:])
                sh_scr[part, r, 0:(groups - 1) * sub, :] = y.reshape((groups - 1) * sub, lanes_w)
            for r0 in range(0, tm, CONV_ROWS):
                acc = None
                for k in range(CONV_WIDTH):
                    a, r = divmod(shift + k, sub)
                    rows = pl.ds(r0 + a * sub, CONV_ROWS)
                    seg = ext_scr[rows, lanes] if r == 0 else sh_scr[part, r, rows, :]
                    term = seg * wdw_ref[pl.ds(k, 1), lanes]
                    acc = term if acc is None else acc + term
                c_scr[pl.ds(r0, CONV_ROWS), lanes] = acc + bdw_ref[:, lanes]

    def out_proj(cb):
        cols = pl.ds(cb * CONV_CB, CONV_CB)
        o_ref[:, cols] = (xp_ref[:, cols] + bo_ref[:, cols]
                          + jnp.dot(s_scr[...], wo_ref[:, cols], preferred_element_type=F32))

    @pl.when(i < last)
    def _():
        ext_scr[0:CONV_HALO, :] = jnp.where(first, 0.0, ext_scr[tm:, :])
        h_scr[...] = _rms_norm_bf16(x_ref[...], g_ref[...])
        in_proj(0)
        for cb in range(nblk):
            if cb + 1 < nblk:
                in_proj(cb + 1)
            conv(cb)
            out_proj(cb)

        c = c_scr[...]
        mu = jnp.mean(c, axis=-1, keepdims=True)
        cc = c - mu
        var = jnp.mean(cc * cc, axis=-1, keepdims=True)
        y = cc * lax.rsqrt(var + EPS) * lng_ref[...] + lnb_ref[...]
        s_scr[...] = (y * _sigmoid(y)).astype(BF16)

    @pl.when(i == last)
    def _():
        for cb in range(nblk):
            out_proj(cb)


def _conv_module(x, g, w_in, b_in, w_dw, b_dw, ln_g, ln_b, w_out, b_out, layer, seq):
    t, d = x.shape
    tm = TM_CONV
    n = t // tm
    cur = lambda i: (jnp.minimum(i, n - 1), 0)
    prev = lambda i: (jnp.maximum(i - 1, 0), 0)
    fixed = lambda i: (layer, 0, 0)
    once = pl.Buffered(1)
    return pl.pallas_call(
        functools.partial(_conv_module_kernel, tiles_per_seq=seq // tm),
        grid=(n + 1,),
        in_specs=[
            pl.BlockSpec((tm, d), cur),
            pl.BlockSpec((tm, d), prev),
            pl.BlockSpec((1, d), lambda i: (0, 0)),
            pl.BlockSpec((None, d, 2 * d), fixed, pipeline_mode=once),
            pl.BlockSpec((None, 1, 2 * d), fixed),
            pl.BlockSpec((None, CONV_WIDTH, d), fixed),
            pl.BlockSpec((None, 1, d), fixed),
            pl.BlockSpec((None, 1, d), fixed),
            pl.BlockSpec((None, 1, d), fixed),
            pl.BlockSpec((None, d, d), fixed, pipeline_mode=once),
            pl.BlockSpec((None, 1, d), fixed),
        ],
        out_specs=pl.BlockSpec((tm, d), prev),
        out_shape=jax.ShapeDtypeStruct((t, d), F32),
        scratch_shapes=[
            pltpu.VMEM((tm, d), BF16),
            pltpu.VMEM((tm + CONV_HALO, d), F32),
            pltpu.VMEM((CONV_CB // V7X_LANES, V7X_SUBLANES, tm + CONV_HALO, V7X_LANES), F32),
            pltpu.VMEM((tm, d), F32),
            pltpu.VMEM((tm, d), BF16),
        ],
        compiler_params=_params("arbitrary"),
        name="conv_module",
    )(x, x, g, w_in, b_in, w_dw, b_dw, ln_g, ln_b, w_out, b_out)


def _ffn_kernel(x_hbm, g_ref, wa_ref, wg_ref, wd_ref, fg_ref, o_ref, x_buf, h_scr, x_sem, *, final_norm):
    i, j = pl.program_id(0), pl.program_id(1)
    tm = x_buf.shape[0]

    def x_copy(tile):
        rows = pl.ds(pl.multiple_of(tile * tm, tm), tm)
        return pltpu.make_async_copy(x_hbm.at[rows], x_buf, x_sem)

    def partial_out(h):
        a = jnp.dot(h, wa_ref[...], preferred_element_type=F32)
        g = jnp.dot(h, wg_ref[...], preferred_element_type=F32)
        p = (a * _sigmoid(a) * g).astype(BF16)
        return jnp.dot(p, wd_ref[...], preferred_element_type=F32)

    @pl.when(j == 0)
    def _():
        @pl.when(i == 0)
        def _():
            x_copy(i).start()

        x_copy(i).wait()
        for r0 in range(0, tm, FFN_ROWS):
            rows = slice(r0, r0 + FFN_ROWS)
            x = x_buf[rows, :]
            h = _rms_norm_bf16(x, g_ref[...])
            h_scr[rows, :] = h
            o_ref[rows, :] = x + partial_out(h)

    @pl.when((j == 1) & (i + 1 < pl.num_programs(0)))
    def _():
        x_copy(i + 1).start()

    @pl.when(j > 0)
    def _():
        o_ref[...] += partial_out(h_scr[...])

    if final_norm:
        @pl.when(j == pl.num_programs(1) - 1)
        def _():
            y = o_ref[...]
            ms = jnp.mean(y * y, axis=-1, keepdims=True)
            o_ref[...] = y * lax.rsqrt(ms + EPS) * fg_ref[...]


def _ffn(x, g, w_gate_up, w_down, final_g, layer, final_norm):
    t, d = x.shape
    d_ff = w_down.shape[1]
    tm, tf = TM_FFN, TN_PROJ
    nj = d_ff // tf
    assert nj >= 2
    return pl.pallas_call(
        functools.partial(_ffn_kernel, final_norm=final_norm),
        grid=(t // tm, nj),
        in_specs=[
            pl.BlockSpec(memory_space=pl.ANY),
            pl.BlockSpec((None, 1, d), lambda i, j: (layer, 0, 0)),
            pl.BlockSpec((None, d, tf), lambda i, j: (layer, 0, j)),
            pl.BlockSpec((None, d, tf), lambda i, j: (layer, 0, j + nj)),
            pl.BlockSpec((None, tf, d), lambda i, j: (layer, j, 0)),
            pl.BlockSpec((1, d), lambda i, j: (0, 0)),
        ],
        out_specs=pl.BlockSpec((tm, d), lambda i, j: (i, 0)),
        out_shape=jax.ShapeDtypeStruct((t, d), F32),
        scratch_shapes=[pltpu.VMEM((tm, d), F32), pltpu.VMEM((tm, d), BF16), pltpu.SemaphoreType.DMA(())],
        compiler_params=_params("arbitrary", "arbitrary"),
        name="ffn_swiglu",
    )(x, g, w_gate_up, w_gate_up, w_down, final_g)


def _gla_proj_kernel(x_ref, g_ref, w_ref, w1_ref, w2_ref, gb_ref, hg_ref, proj_ref, la_ref, h_scr, *,
                     gate_tile0):
    j = pl.program_id(1)

    @pl.when(j == 0)
    def _():
        for r0 in range(0, x_ref.shape[0], GATE_ROWS):
            rows = slice(r0, r0 + GATE_ROWS)
            h = _rms_norm_bf16(x_ref[rows, :], g_ref[...])
            h_scr[rows, :] = h
            low = jnp.dot(h, w1_ref[...], preferred_element_type=F32).astype(BF16)
            z = jnp.dot(low, w2_ref[...], preferred_element_type=F32) + gb_ref[...]
            log_sig = jnp.minimum(z, 0.0) - jnp.log(1.0 + jnp.exp(-jnp.abs(z)))
            la_ref[rows, :] = log_sig * (LOG2_E / GLA_GATE_NORMALIZER)

    h = h_scr[...]
    for c0 in range(0, w_ref.shape[1], V7X_MXU_COLS):
        cols = slice(c0, c0 + V7X_MXU_COLS)
        acc = jnp.dot(h, w_ref[:, cols], preferred_element_type=F32)
        gated = acc * _sigmoid(acc) * hg_ref[:, cols]
        proj_ref[:, cols] = jnp.where(j >= gate_tile0, gated, acc).astype(BF16)


def _gla_proj(x, g, w_in, w1, w2, gate_b, head_g, layer):
    t, d = x.shape
    n = w_in.shape[2]
    dk = w2.shape[2]
    rank = w1.shape[2]
    dv = head_g.shape[2]
    tm, tn = TM_PROJ, TN_GLA_PROJ
    gate_tile0 = (n - dv) // tn
    fixed = lambda i, j: (layer, 0, 0)
    return pl.pallas_call(
        functools.partial(_gla_proj_kernel, gate_tile0=gate_tile0),
        grid=(t // tm, n // tn),
        in_specs=[
            pl.BlockSpec((tm, d), lambda i, j: (i, 0)),
            pl.BlockSpec((1, d), lambda i, j: (0, 0)),
            pl.BlockSpec((None, d, tn), lambda i, j: (layer, 0, j)),
            pl.BlockSpec((None, d, rank), fixed),
            pl.BlockSpec((None, rank, dk), fixed),
            pl.BlockSpec((None, 1, dk), fixed),
            pl.BlockSpec((None, 1, tn), lambda i, j: (layer, 0, jnp.maximum(j - gate_tile0, 0))),
        ],
        out_specs=[
            pl.BlockSpec((tm, tn), lambda i, j: (i, j)),
            pl.BlockSpec((tm, dk), lambda i, j: (i, 0)),
        ],
        out_shape=[jax.ShapeDtypeStruct((t, n), BF16), jax.ShapeDtypeStruct((t, dk), F32)],
        scratch_shapes=[pltpu.VMEM((tm, d), BF16)],
        compiler_params=_params("parallel", "arbitrary"),
        name="gla_in_proj",
    )(x, g, w_in, w1, w2, gate_b, head_g)


def _gla_decay_matrix():
    c = GLA_CHUNK
    i = np.arange(c)[:, None]
    t = np.arange(c)[None, :]
    blocks = [t <= i, t > i]
    for b in GLA_LEVELS:
        ref = (i // (2 * b)) * (2 * b) + b - 1
        blocks.append(np.where(i > ref, (t > ref) & (t <= i), (t > i) & (t <= ref)))
    a = np.concatenate(blocks, axis=0).astype(np.float32)
    return np.concatenate([a, a], axis=1)


def _gla_kernel(qk_ref, v_ref, rg_ref, la_ref, amat_ref, x_ref, wo_ref, o_ref, s_scr, og_scr):
    c = GLA_CHUNK
    lt = qk_ref.shape[0]
    heads = s_scr.shape[0]
    dk, dv = s_scr.shape[1], s_scr.shape[2]
    k_base = heads * dk

    @pl.when(pl.program_id(1) == 0)
    def _():
        s_scr[...] = jnp.zeros_like(s_scr)

    ri = lax.broadcasted_iota(jnp.int32, (c, c), 0)
    ci = lax.broadcasted_iota(jnp.int32, (c, c), 1)
    masks = [ri == ci]
    for b in GLA_LEVELS:
        masks.append((ri // (2 * b) == ci // (2 * b)) & ((ri // b) % 2 == 1) & ((ci // b) % 2 == 0))
    amat = amat_ref[...]
    nt = (((1,), (1,)), ((), ()))

    def chunk(ic, carry):
        rows = pl.ds(pl.multiple_of(ic * c, c), c)
        qcols = [slice(h * dk, (h + 1) * dk) for h in range(heads)]
        kcols = [slice(k_base + h * dk, k_base + (h + 1) * dk) for h in range(heads)]
        vcols = [slice(h * dv, (h + 1) * dv) for h in range(heads)]
        las = [la_ref[rows, qcols[h]] for h in range(heads)]
        e_rem, es = [], []
        for h in range(heads):
            la_hi = las[h].astype(BF16)
            la_lo = (las[h] - la_hi.astype(F32)).astype(BF16)
            expo = jnp.dot(amat, jnp.concatenate([la_hi, la_lo], axis=0), preferred_element_type=F32)
            e = jnp.exp2(expo)
            e_rem.append(e[c:2 * c])
            es.append(e.astype(BF16))
        qs = [qk_ref[rows, qcols[h]] for h in range(heads)]
        ks = [qk_ref[rows, kcols[h]] for h in range(heads)]
        scores = [jnp.where(masks[0], lax.dot_general(qs[h], ks[h], nt, preferred_element_type=F32), 0.0)
                  for h in range(heads)]
        for lvl in range(len(GLA_LEVELS)):
            for h in range(heads):
                eb = es[h][(2 + lvl) * c:(3 + lvl) * c]
                sb = lax.dot_general(qs[h] * eb, ks[h] * eb, nt, preferred_element_type=F32)
                scores[h] = scores[h] + jnp.where(masks[1 + lvl], sb, 0.0)
        for h in range(heads):
            v = v_ref[rows, vcols[h]]
            state = s_scr[h]
            o = jnp.dot(qs[h] * es[h][0:c], state.astype(BF16), preferred_element_type=F32)
            o = o + jnp.dot(scores[h].astype(BF16), v, preferred_element_type=F32)
            k_rem_t = (ks[h].astype(F32) * e_rem[h]).T.astype(BF16)
            e_last = jnp.exp2(jnp.sum(las[h].T, axis=1, keepdims=True))
            s_scr[h] = state * e_last + jnp.dot(k_rem_t, v, preferred_element_type=F32)
            ms = jnp.mean(o * o, axis=-1, keepdims=True)
            gate = rg_ref[rows, vcols[h]].astype(F32)
            og_scr[rows, vcols[h]] = (o * lax.rsqrt(ms + EPS * dk) * gate).astype(BF16)
        return carry

    lax.fori_loop(0, lt // c, chunk, 0, unroll=2)
    o_ref[...] = x_ref[...] + jnp.dot(og_scr[...], wo_ref[...], preferred_element_type=F32)


def _gla(proj, log_a, x, w_out, layer, batch, seq):
    t, d = x.shape
    dk_total = log_a.shape[1]
    dv_total = (proj.shape[1] - 2 * dk_total) // 2
    assert 2 * dk_total == dv_total
    lt = GLA_LT
    steps = seq // lt
    amat = jnp.asarray(_gla_decay_matrix(), BF16)
    row = lambda b, s: b * steps + s
    return pl.pallas_call(
        _gla_kernel,
        grid=(batch, steps),
        in_specs=[
            pl.BlockSpec((lt, 2 * dk_total), lambda b, s: (row(b, s), 0)),
            pl.BlockSpec((lt, dv_total), lambda b, s: (row(b, s), 1)),
            pl.BlockSpec((lt, dv_total), lambda b, s: (row(b, s), 2)),
            pl.BlockSpec((lt, dk_total), lambda b, s: (row(b, s), 0)),
            pl.BlockSpec(amat.shape, lambda b, s: (0, 0)),
            pl.BlockSpec((lt, d), lambda b, s: (row(b, s), 0)),
            pl.BlockSpec((None, dv_total, d), lambda b, s: (layer, 0, 0), pipeline_mode=pl.Buffered(1)),
        ],
        out_specs=pl.BlockSpec((lt, d), lambda b, s: (row(b, s), 0)),
        out_shape=jax.ShapeDtypeStruct((t, d), F32),
        scratch_shapes=[
            pltpu.VMEM((GLA_HEADS, dk_total // GLA_HEADS, dv_total // GLA_HEADS), F32),
            pltpu.VMEM((lt, dv_total), BF16),
        ],
        compiler_params=_params("parallel", "arbitrary"),
        name="gla_chunked",
    )(proj, proj, proj, log_a, amat, x, w_out)


def kernel(x, norm_mix_g, norm_ffn_g, conv_w_in, conv_b_in, conv_w_dw, conv_b_dw, conv_ln_g, conv_ln_b,
           conv_w_out, conv_b_out, gla_w_in, gla_gate_w1, gla_gate_w2, gla_gate_b, gla_head_norm_g,
           gla_w_out, ffn_w_gate_up, ffn_w_down, final_norm_g):
    batch, seq, d = x.shape
    depth = norm_mix_g.shape[0]
    assert all(seq % tile == 0 for tile in (TM_PROJ, TM_FFN, TM_CONV, GLA_LT))
    rows3d = lambda a: a.reshape(a.shape[0], 1, -1)
    conv_w_in, conv_w_out = conv_w_in.astype(BF16), conv_w_out.astype(BF16)
    gla_w_in, gla_w_out = gla_w_in.astype(BF16), gla_w_out.astype(BF16)
    gla_gate_w1, gla_gate_w2 = gla_gate_w1.astype(BF16), gla_gate_w2.astype(BF16)
    ffn_w_gate_up, ffn_w_down = ffn_w_gate_up.astype(BF16), ffn_w_down.astype(BF16)
    norm_ffn_g3 = rows3d(norm_ffn_g)
    final_g = final_norm_g.reshape(1, -1)
    h = x.reshape(batch * seq, d)
    for i in range(depth):
        j = i // 2
        g_mix = norm_mix_g[i].reshape(1, -1)
        if i % 2 == 0:
            h = _conv_module(h, g_mix, conv_w_in, rows3d(conv_b_in), conv_w_dw, rows3d(conv_b_dw),
                             rows3d(conv_ln_g), rows3d(conv_ln_b), conv_w_out, rows3d(conv_b_out), j, seq)
        else:
            proj, log_a = _gla_proj(h, g_mix, gla_w_in, gla_gate_w1, gla_gate_w2, rows3d(gla_gate_b),
                                    rows3d(gla_head_norm_g), j)
            h = _gla(proj, log_a, h, gla_w_out, j, batch, seq)
        h = _ffn(h, norm_ffn_g3, ffn_w_gate_up, ffn_w_down, final_g, i, final_norm=(i == depth - 1))
    return h.reshape(batch, seq, d)
```
